```python
import jax, jax.numpy as jnp
from jax import lax
import numpy as np

D_MODEL = 1024
BATCH = 1
SEQ = 16384
DEPTH = 1
DEC_BATCH = 16
DEC_SEQ = 2048
PAST_LEN = 128

HEAD_DIM = 64
RWKV_WIDTH = D_MODEL // 2
RWKV_HEADS = RWKV_WIDTH // HEAD_DIM
NA_WIDTH = D_MODEL // 2
NA_HEADS = NA_WIDTH // HEAD_DIM
DECAY_LORA = 64
AAA_LORA = 64
GATE_LORA = 128
GRID_W = 64
NA_ROWS = 8
NA_COLS = 16
D_FF = -(-8 * D_MODEL // (3 * 256)) * 256
PLE_DIM = 256
NORM_EPS = 1e-6
GN_EPS = 64e-5
RWKV_SIZES = (RWKV_WIDTH, RWKV_WIDTH, RWKV_WIDTH, DECAY_LORA, DECAY_LORA, AAA_LORA, AAA_LORA, GATE_LORA)
RWKV_SPLITS = tuple(int(s) for s in np.cumsum(RWKV_SIZES)[:-1])
RWKV_IN = sum(RWKV_SIZES)
NA_IN = 3 * NA_WIDTH
GATE_IN = 2 * D_MODEL
D_IN = RWKV_IN + NA_IN + GATE_IN

kernel_name = 'hybrid_rwkv7_natten_encoder'


def rmsnorm(x, g):
    xf = x.astype(jnp.float32)
    y = xf * lax.rsqrt(jnp.mean(xf * xf, axis=-1, keepdims=True) + NORM_EPS)
    return (y * g.astype(jnp.float32)).astype(x.dtype)


def centred_shift(z, mu_prev, mu_next):
    zp = jnp.pad(z[:, :-1], ((0, 0), (1, 0), (0, 0)))
    zn = jnp.pad(z[:, 1:], ((0, 0), (0, 1), (0, 0)))
    return z + mu_prev * (zp - z) + mu_next * (zn - z)


def wkv7_scan(r, w, k, v, a, b, reverse):
    B, T, H, N = r.shape
    xs = tuple(jnp.moveaxis(t, 1, 0) for t in (r, w, k, v, a, b))

    def step(S, inp):
        r_t, w_t, k_t, v_t, a_t, b_t = inp
        S_new = (S * w_t[:, :, None, :]
                 + jnp.einsum('bhvk,bhk->bhv', S, a_t)[..., None] * b_t[:, :, None, :]
                 + v_t[..., None] * k_t[:, :, None, :])
        y = jnp.einsum('bhvk,bhk->bhv', S if reverse else S_new, r_t)
        return S_new, y

    S0 = jnp.zeros((B, H, N, N), jnp.float32)
    _, ys = lax.scan(step, S0, xs, reverse=reverse)
    return jnp.moveaxis(ys, 0, 1)


def rwkv7_branch(z, L):
    B, T, _ = z.shape
    f32 = jnp.float32
    z = centred_shift(z, L['mu_prev'], L['mu_next'])
    r, k, v, wd_f, wd_b, ad_f, ad_b, gd = jnp.split(z, RWKV_SPLITS, axis=-1)
    r, k, v = r.astype(f32), k.astype(f32), v.astype(f32)

    def decay(wd, w0, w2):
        w = -jax.nn.softplus(-(w0.astype(f32) + (jnp.tanh(wd) @ w2).astype(f32))) - 0.5
        return jnp.exp(-jnp.exp(w))

    def rate(ad, a0, a2):
        return jax.nn.sigmoid(a0.astype(f32) + (ad @ a2).astype(f32))

    w_f = decay(wd_f, L['w0_f'], L['w2_f'])
    w_b = decay(wd_b, L['w0_b'], L['w2_b'])
    a_f = rate(ad_f, L['a0_f'], L['a2_f'])
    a_b = rate(ad_b, L['a0_b'], L['a2_b'])
    g = (jax.nn.sigmoid(gd) @ L['g2']).astype(f32)
    k_a = L['k_a'].astype(f32)
    k_f = k * (1.0 + (a_f - 1.0) * k_a)
    k_b = k * (1.0 + (a_b - 1.0) * k_a)

    hd = lambda t: t.reshape(B, T, RWKV_HEADS, HEAD_DIM)
    kk = hd(k * L['k_k'].astype(f32))
    kk = kk / jnp.maximum(jnp.sqrt(jnp.sum(kk * kk, axis=-1, keepdims=True)), 1e-12)
    rh, vh = hd(r), hd(v)
    o = (wkv7_scan(rh, hd(w_f), hd(k_f), vh, -kk, kk * hd(a_f), False)
         + wkv7_scan(rh, hd(w_b), hd(k_b), vh, -kk, kk * hd(a_b), True))
    mu = jnp.mean(o, axis=-1, keepdims=True)
    var = jnp.mean(jnp.square(o - mu), axis=-1, keepdims=True)
    o = (o - mu) * lax.rsqrt(var + GN_EPS)
    o = o.reshape(B, T, RWKV_WIDTH) * L['lnx_w'].astype(f32) + L['lnx_b'].astype(f32)
    bonus = jnp.sum(rh * hd(0.5 * (k_f + k_b)) * L['r_k'].astype(f32), axis=-1, keepdims=True) * vh
    o = o + bonus.reshape(B, T, RWKV_WIDTH)
    return (o * g).astype(z.dtype)


def neighbourhood_attention(q, k, v, rpb):
    B, T, _ = q.shape
    rows = T // GRID_W
    kh = min(NA_ROWS, rows)
    shp = (B, rows, GRID_W, NA_HEADS, HEAD_DIM)
    q, k, v = q.reshape(shp), k.reshape(shp), v.reshape(shp)
    row_start = jnp.clip(jnp.arange(rows) - kh // 2, 0, rows - kh)
    cols = jnp.arange(GRID_W)
    col_idx = (jnp.clip(cols - NA_COLS // 2, 0, GRID_W - NA_COLS)[:, None]
               + jnp.arange(NA_COLS)[None, :])
    dc_idx = col_idx - cols[:, None] + (NA_COLS - 1)
    col_bias = rpb.astype(jnp.float32)[:, :, dc_idx]
    scale = HEAD_DIM ** -0.5

    def row_block(r):
        rs = row_start[r]
        kr = lax.dynamic_slice_in_dim(k, rs, kh, axis=1)[:, :, col_idx]
        vr = lax.dynamic_slice_in_dim(v, rs, kh, axis=1)[:, :, col_idx]
        qr = lax.dynamic_index_in_dim(q, r, axis=1, keepdims=False)
        s = jnp.einsum('bwhd,biwjhd->bhwij', qr, kr).astype(jnp.float32) * scale
        dr = rs + jnp.arange(kh) - r + (NA_ROWS - 1)
        bias = jnp.take(col_bias, dr, axis=1)
        s = s + jnp.transpose(bias, (0, 2, 1, 3))[None]
        p = jax.nn.softmax(s.reshape(B, NA_HEADS, GRID_W, kh * NA_COLS), axis=-1)
        p = p.reshape(s.shape).astype(v.dtype)
        return jnp.einsum('bhwij,biwjhd->bwhd', p, vr)

    out = lax.map(row_block, jnp.arange(rows))
    return jnp.moveaxis(out, 0, 1).reshape(B, T, NA_WIDTH)


def encoder_layer(x, p, L):
    h = rmsnorm(x, L['g_mix'])
    z = h @ L['w_in']
    z_rwkv = z[..., :RWKV_IN]
    z_na = z[..., RWKV_IN:RWKV_IN + NA_IN]
    z_gate = z[..., RWKV_IN + NA_IN:]
    u_a = rwkv7_branch(z_rwkv, L)
    q, k, v = jnp.split(z_na, 3, axis=-1)
    u_n = neighbourhood_attention(q, k, v, L['rpb'])
    gate_a, gate_n = jnp.split(jax.nn.sigmoid(z_gate), 2, axis=-1)
    m = gate_a * (u_a @ L['w_br_a']) + gate_n * (u_n @ L['w_br_n'])
    x = x + m @ L['w_out']
    h = rmsnorm(x, L['g_ffn'])
    x = x + (jax.nn.silu(h @ L['w_gate']) * (h @ L['w_up'])) @ L['w_down']
    x = x + (p @ L['w_ple']) * jax.nn.sigmoid(rmsnorm(x, L['g_ple']) @ L['w_pg'])
    return x


def trunk(x, p, layers, g_final):
    for i in range(DEPTH):
        L = {name: arr[i] for name, arr in layers.items()}
        x = encoder_layer(x, p[i], L)
    return rmsnorm(x, g_final)


def setup_inputs(seed: int = 0) -> dict:
    key = jax.random.key(seed)
    ks = iter(jax.random.split(key, 40))

    def nrm(shape, scale):
        return scale * jax.random.normal(next(ks), shape, jnp.float32)

    def gain(shape):
        return 1.0 + nrm(shape, 0.02)

    def unif(shape, lo, hi):
        return jax.random.uniform(next(ks), shape, jnp.float32, lo, hi)

    return {
        'x_prompt': nrm((BATCH, SEQ, D_MODEL), 1.0),
        'x_sample': nrm((DEC_BATCH, DEC_SEQ, D_MODEL), 1.0),
        'p_prompt': nrm((DEPTH, BATCH, SEQ, PLE_DIM), 1.0),
        'p_sample': nrm((DEPTH, DEC_BATCH, DEC_SEQ, PLE_DIM), 1.0),
        'g_mix': gain((DEPTH, D_MODEL)),
        'w_in': nrm((DEPTH, D_MODEL, D_IN), D_MODEL ** -0.5),
        'mu_prev': unif((DEPTH, RWKV_IN), 0.0, 0.5),
        'mu_next': unif((DEPTH, RWKV_IN), 0.0, 0.5),
        'w0_f': unif((DEPTH, RWKV_WIDTH), -5.0, -1.0),
        'w2_f': nrm((DEPTH, DECAY_LORA, RWKV_WIDTH), 0.3 * DECAY_LORA ** -0.5),
        'w0_b': unif((DEPTH, RWKV_WIDTH), -5.0, -1.0),
        'w2_b': nrm((DEPTH, DECAY_LORA, RWKV_WIDTH), 0.3 * DECAY_LORA ** -0.5),
        'a0_f': nrm((DEPTH, RWKV_WIDTH), 0.1),
        'a2_f': nrm((DEPTH, AAA_LORA, RWKV_WIDTH), 0.5 * AAA_LORA ** -0.5),
        'a0_b': nrm((DEPTH, RWKV_WIDTH), 0.1),
        'a2_b': nrm((DEPTH, AAA_LORA, RWKV_WIDTH), 0.5 * AAA_LORA ** -0.5),
        'g2': nrm((DEPTH, GATE_LORA, RWKV_WIDTH), GATE_LORA ** -0.5),
        'k_k': 0.85 + nrm((DEPTH, RWKV_WIDTH), 0.02),
        'k_a': gain((DEPTH, RWKV_WIDTH)),
        'r_k': nrm((DEPTH, RWKV_HEADS, HEAD_DIM), 0.1),
        'lnx_w': gain((DEPTH, RWKV_WIDTH)),
        'lnx_b': nrm((DEPTH, RWKV_WIDTH), 0.01),
        'rpb': nrm((DEPTH, NA_HEADS, 2 * NA_ROWS - 1, 2 * NA_COLS - 1), 0.1),
        'w_br_a': nrm((DEPTH, RWKV_WIDTH, D_MODEL), RWKV_WIDTH ** -0.5),
        'w_br_n': nrm((DEPTH, NA_WIDTH, D_MODEL), NA_WIDTH ** -0.5),
        'w_out': nrm((DEPTH, D_MODEL, D_MODEL), D_MODEL ** -0.5),
        'g_ffn': gain((DEPTH, D_MODEL)),
        'w_gate': nrm((DEPTH, D_MODEL, D_FF), D_MODEL ** -0.5),
        'w_up': nrm((DEPTH, D_MODEL, D_FF), D_MODEL ** -0.5),
        'w_down': nrm((DEPTH, D_FF, D_MODEL), D_FF ** -0.5),
        'g_ple': gain((DEPTH, D_MODEL)),
        'w_ple': nrm((DEPTH, PLE_DIM, D_MODEL), PLE_DIM ** -0.5),
        'w_pg': nrm((DEPTH, D_MODEL, D_MODEL), D_MODEL ** -0.5),
        'g_final': gain((D_MODEL,)),
    }


def reference(x_prompt, x_sample, p_prompt, p_sample, g_mix, w_in, mu_prev, mu_next,
              w0_f, w2_f, w0_b, w2_b, a0_f, a2_f, a0_b, a2_b, g2, k_k, k_a, r_k,
              lnx_w, lnx_b, rpb, w_br_a, w_br_n, w_out, g_ffn, w_gate, w_up, w_down,
              g_ple, w_ple, w_pg, g_final):
    layers = {
        'g_mix': g_mix, 'w_in': w_in, 'mu_prev': mu_prev, 'mu_next': mu_next,
        'w0_f': w0_f, 'w2_f': w2_f, 'w0_b': w0_b, 'w2_b': w2_b,
        'a0_f': a0_f, 'a2_f': a2_f, 'a0_b': a0_b, 'a2_b': a2_b,
        'g2': g2, 'k_k': k_k, 'k_a': k_a, 'r_k': r_k, 'lnx_w': lnx_w, 'lnx_b': lnx_b,
        'rpb': rpb, 'w_br_a': w_br_a, 'w_br_n': w_br_n, 'w_out': w_out,
        'g_ffn': g_ffn, 'w_gate': w_gate, 'w_up': w_up, 'w_down': w_down,
        'g_ple': g_ple, 'w_ple': w_ple, 'w_pg': w_pg,
    }
    y_prompt = trunk(x_prompt, p_prompt, layers, g_final)
    y_sample = trunk(x_sample, p_sample, layers, g_final)
    return (y_prompt, y_sample)
```

```python
import functools

import numpy as np
import jax
import jax.numpy as jnp
from jax import lax
from jax.experimental import pallas as pl
from jax.experimental.pallas import tpu as pltpu

F32 = jnp.float32
BF16 = jnp.bfloat16

D_MODEL = 1024
HEAD_DIM = 64
RWKV_WIDTH = 512
NA_WIDTH = 512
N_HEADS = 8
RWKV_IN = 1920
NA_IN = 1536
GATE_IN = 2048
D_FF = 2816
PLE_DIM = 256
GRID_W = 64
NA_ROWS = 8
NA_COLS = 16
NORM_EPS = 1e-6
GN_EPS = 64e-5

CHUNK = 64
QUAD = 4 * HEAD_DIM
TOKEN_TILE = 256
SCAN_BLOCK = 256
NA_QROWS = 4
NA_KROWS = 12
NEG_BIG = -1e30
VMEM_LIMIT = 56 * 1024 * 1024


def _dot(a, b):
    return jnp.dot(a, b, preferred_element_type=F32)


def _dot_nt(a, b):
    return lax.dot_general(a, b, (((1,), (1,)), ((), ())), preferred_element_type=F32)


def _dot_tn(a, b):
    return lax.dot_general(a, b, (((0,), (0,)), ((), ())), preferred_element_type=F32)


def _split_dot(x, m):
    hi = x.astype(BF16)
    lo = (x - hi.astype(F32)).astype(BF16)
    return _dot(hi, m) + _dot(lo, m)


def _split3_dot(m, x):
    hi = x.astype(BF16)
    r1 = x - hi.astype(F32)
    mid = r1.astype(BF16)
    lo = (r1 - mid.astype(F32)).astype(BF16)
    return _dot(m, hi) + _dot(m, mid) + _dot(m, lo)


def _sigmoid(x):
    return 1.0 / (1.0 + jnp.exp(-x))


def _rms(x, g):
    ms = jnp.mean(x * x, axis=-1, keepdims=True)
    return x * lax.rsqrt(ms + NORM_EPS) * g


def _const_spec(shape):
    nd = len(shape)
    return pl.BlockSpec(shape, lambda *_: (0,) * nd)


def _params(sem):
    return pltpu.CompilerParams(dimension_semantics=sem, vmem_limit_bytes=VMEM_LIMIT)


def _in_proj_kernel(x_ref, g_ref, wr_ref, wn_ref, wg_ref, zr_ref, qkv_ref, gate_ref):
    h = _rms(x_ref[...], g_ref[...]).astype(BF16)
    zr_ref[...] = _dot(h, wr_ref[...])
    qkv_ref[...] = _dot(h, wn_ref[...]).astype(BF16)
    gate_ref[...] = _sigmoid(_dot(h, wg_ref[...])).astype(BF16)


def _in_proj(x2, g_mix, w_r, w_n, w_g):
    n = x2.shape[0]
    tm = TOKEN_TILE
    row = lambda w: pl.BlockSpec((tm, w), lambda i: (i, 0))
    return pl.pallas_call(
        _in_proj_kernel,
        grid=(n // tm,),
        in_specs=[row(D_MODEL), _const_spec((1, D_MODEL)), _const_spec((D_MODEL, RWKV_IN)),
                  _const_spec((D_MODEL, NA_IN)), _const_spec((D_MODEL, GATE_IN))],
        out_specs=[row(RWKV_IN), row(NA_IN), row(GATE_IN)],
        out_shape=[jax.ShapeDtypeStruct((n, RWKV_IN), F32),
                   jax.ShapeDtypeStruct((n, NA_IN), BF16),
                   jax.ShapeDtypeStruct((n, GATE_IN), BF16)],
        compiler_params=_params(("parallel",)),
        name="in_proj",
    )(x2, g_mix, w_r, w_n, w_g)


def _rwkv_prep_kernel(z_ref, zp_ref, zn_ref, mup_ref, mun_ref, vec_ref, wlora_ref, alora_ref,
                      g2_ref, seg_ref, cumf_ref, cumb_ref,
                      r_ref, v_ref, kk_ref, linf_ref, lexf_ref, kf_ref, bf_ref,
                      linb_ref, lexb_ref, kb_ref, bb_ref, bonus_ref, g_ref):
    i = pl.program_id(1)
    n_i = pl.num_programs(1)
    tm = z_ref.shape[0]
    z = z_ref[...]
    prev_row = jnp.where(i > 0, zp_ref[7:8, :], 0.0)
    next_row = jnp.where(i < n_i - 1, zn_ref[0:1, :], 0.0)
    rows = lax.broadcasted_iota(jnp.int32, (tm, 1), 0)
    zp = jnp.where(rows == 0, prev_row, pltpu.roll(z, 1, 0))
    zn = jnp.where(rows == tm - 1, next_row, pltpu.roll(z, tm - 1, 0))
    zs = z + mup_ref[...] * (zp - z) + mun_ref[...] * (zn - z)

    w = RWKV_WIDTH
    r, k, v = zs[:, 0:w], zs[:, w:2 * w], zs[:, 2 * w:3 * w]
    wd = zs[:, 3 * w:3 * w + 128]
    ad = zs[:, 3 * w + 128:3 * w + 256]
    gd = zs[:, 3 * w + 256:3 * w + 384]

    w0_f, w0_b = vec_ref[0:1, :], vec_ref[1:2, :]
    a0_f, a0_b = vec_ref[2:3, :], vec_ref[3:4, :]
    k_k, k_a, r_k = vec_ref[4:5, :], vec_ref[5:6, :], vec_ref[6:7, :]

    wl = _dot(jnp.tanh(wd).astype(BF16), wlora_ref[...])
    al = _dot(ad.astype(BF16), alora_ref[...])

    def log_decay(w0, lora):
        y = -(w0 + lora)
        softplus = jnp.maximum(y, 0.0) + jnp.log(1.0 + jnp.exp(-jnp.abs(y)))
        return -jnp.exp(-softplus - 0.5)

    lw_f = log_decay(w0_f, wl[:, 0:w])
    lw_b = log_decay(w0_b, wl[:, w:2 * w])
    a_f = _sigmoid(a0_f + al[:, 0:w])
    a_b = _sigmoid(a0_b + al[:, w:2 * w])
    g_ref[...] = _dot(_sigmoid(gd).astype(BF16), g2_ref[...])

    k_f = k * (1.0 + (a_f - 1.0) * k_a)
    k_b = k * (1.0 + (a_b - 1.0) * k_a)
    kk = k * k_k
    seg = seg_ref[...]
    ss = _split_dot(kk * kk, seg)
    kk = kk / jnp.maximum(jnp.sqrt(ss), 1e-12)
    bonus_ref[...] = _split_dot(r * (0.5 * (k_f + k_b)) * r_k, seg) * v

    lin_f = _split3_dot(cumf_ref[...], lw_f)
    lin_b = _split3_dot(cumb_ref[...], lw_b)
    r_ref[...] = r
    v_ref[...] = v
    kk_ref[...] = kk
    linf_ref[...] = lin_f
    lexf_ref[...] = lin_f - lw_f
    kf_ref[...] = k_f
    bf_ref[...] = kk * a_f
    linb_ref[...] = lin_b
    lexb_ref[...] = lin_b - lw_b
    kb_ref[...] = k_b
    bb_ref[...] = kk * a_b


def _rwkv_prep(z3, mu_p, mu_n, vecs, wlora, alora, g2, seg, cum_f, cum_b):
    b, t, _ = z3.shape
    tm = TOKEN_TILE
    nb8 = tm // 8
    tile = lambda w: pl.BlockSpec((None, tm, w), lambda bi, i: (bi, i, 0))
    halo_p = pl.BlockSpec((None, 8, RWKV_IN), lambda bi, i: (bi, jnp.maximum(i * nb8 - 1, 0), 0))
    halo_n = pl.BlockSpec((None, 8, RWKV_IN),
                          lambda bi, i: (bi, jnp.minimum((i + 1) * nb8, t // 8 - 1), 0))
    out = jax.ShapeDtypeStruct((b, t, RWKV_WIDTH), F32)
    return pl.pallas_call(
        _rwkv_prep_kernel,
        grid=(b, t // tm),
        in_specs=[tile(RWKV_IN), halo_p, halo_n, _const_spec((1, RWKV_IN)), _const_spec((1, RWKV_IN)),
                  _const_spec((8, RWKV_WIDTH)), _const_spec((128, 2 * RWKV_WIDTH)),
                  _const_spec((128, 2 * RWKV_WIDTH)), _const_spec((128, RWKV_WIDTH)),
                  _const_spec((RWKV_WIDTH, RWKV_WIDTH)), _const_spec((tm, tm)), _const_spec((tm, tm))],
        out_specs=[tile(RWKV_WIDTH)] * 13,
        out_shape=[out] * 13,
        compiler_params=_params(("parallel", "parallel")),
        name="rwkv_prep",
    )(z3, z3, z3, mu_p, mu_n, vecs, wlora, alora, g2, seg, cum_f, cum_b)


def _block_diag(x, same_head):
    tiled = jnp.concatenate([x, x, x, x], axis=0)
    return jnp.where(same_head, tiled, 0.0).astype(BF16)


def _rwkv_scan_kernel(r_ref, v_ref, kk_ref, lin_ref, lex_ref, k_ref, b_ref, o_ref, s_ref, *, reverse):
    n_chunks = r_ref.shape[0] // CHUNK
    n_quads = RWKV_WIDTH // QUAD

    @pl.when(pl.program_id(1) == 0)
    def _():
        s_ref[...] = jnp.zeros_like(s_ref)

    row_h = lax.broadcasted_iota(jnp.int32, (QUAD, QUAD), 0) >> 6
    col_h = lax.broadcasted_iota(jnp.int32, (QUAD, QUAD), 1) >> 6
    same_head = row_h == col_h
    t_idx = lax.broadcasted_iota(jnp.int32, (CHUNK, QUAD), 0)
    s_idx = lax.broadcasted_iota(jnp.int32, (CHUNK, QUAD), 1) & (CHUNK - 1)
    if reverse:
        strict = s_idx > t_idx
        out_mask = strict
    else:
        strict = s_idx < t_idx
        out_mask = s_idx <= t_idx
    eye = (s_idx == t_idx).astype(F32)

    order = range(n_chunks - 1, -1, -1) if reverse else range(n_chunks)
    for c in order:
        rows = slice(c * CHUNK, (c + 1) * CHUNK)
        tot_row = c * CHUNK if reverse else (c + 1) * CHUNK - 1
        for q in range(n_quads):
            lanes = slice(q * QUAD, (q + 1) * QUAD)
            lin = lin_ref[rows, lanes]
            lex = lex_ref[rows, lanes]
            tot = lin_ref[tot_row:tot_row + 1, lanes]
            e_ex = jnp.exp(lex)
            e_neg = jnp.exp(-lin)
            e_end = jnp.exp(tot - lin)
            r = r_ref[rows, lanes]
            v = v_ref[rows, lanes]
            kk = kk_ref[rows, lanes]
            k = k_ref[rows, lanes]
            b = b_ref[rows, lanes]
            a_t = -kk * e_ex
            r_t = r * (e_ex if reverse else jnp.exp(lin))
            b_t = b * e_neg
            k_t = k * e_neg
            b_h = b * e_end
            k_h = k * e_end

            ar = jnp.concatenate([a_t, r_t], axis=0).astype(BF16)
            sb = _dot_nt(ar, _block_diag(b_t, same_head))
            sk = _dot_nt(ar, _block_diag(k_t, same_head))
            a_ab = jnp.where(strict, sb[:CHUNK], 0.0)
            a_ak = jnp.where(strict, sk[:CHUNK], 0.0)
            m_rb = jnp.where(out_mask, sb[CHUNK:], 0.0)
            m_rk = jnp.where(out_mask, sk[CHUNK:], 0.0)

            p = a_ab
            t_inv = eye + a_ab
            n_sq = CHUNK.bit_length() - 1
            for j in range(n_sq - 1):
                bd = _block_diag(p, same_head)
                if j == 0:
                    p = _dot(p.astype(BF16), bd)
                else:
                    tp = _dot(jnp.concatenate([t_inv, p], axis=0).astype(BF16), bd)
                    t_inv = t_inv + tp[:CHUNK]
                    p = tp[CHUNK:]
            t_inv = t_inv + _dot(t_inv.astype(BF16), _block_diag(p, same_head))

            v_bd = _block_diag(v, same_head)
            akv = _dot(a_ak.astype(BF16), v_bd)
            t_b = t_inv.astype(BF16)
            w_mat = _dot(t_b, _block_diag(a_t, same_head))
            u0 = _dot(t_b, _block_diag(akv, same_head))

            s_old = s_ref[q]
            wr = jnp.concatenate([w_mat, r_t], axis=0).astype(BF16)
            ws = _dot_nt(wr, s_old.astype(BF16))
            u = ws[:CHUNK] + u0
            y = (ws[CHUNK:] + _dot(m_rb.astype(BF16), _block_diag(u, same_head))
                 + _dot(m_rk.astype(BF16), v_bd))
            o_ref[rows, lanes] = y

            uv = jnp.concatenate([u, v], axis=0).astype(BF16)
            bk = jnp.concatenate([b_h, k_h], axis=0).astype(BF16)
            upd = _dot_tn(uv, bk)
            s_ref[q] = s_old * jnp.exp(tot) + jnp.where(same_head, upd, 0.0)


def _rwkv_scan(r, v, kk, lin, lex, k, b, *, reverse):
    bsz, t, _ = r.shape
    tb = SCAN_BLOCK
    nblk = t // tb
    if reverse:
        idx = lambda bi, i: (bi, nblk - 1 - i, 0)
    else:
        idx = lambda bi, i: (bi, i, 0)
    spec = pl.BlockSpec((None, tb, RWKV_WIDTH), idx)
    return pl.pallas_call(
        functools.partial(_rwkv_scan_kernel, reverse=reverse),
        grid=(bsz, nblk),
        in_specs=[spec] * 7,
        out_specs=spec,
        out_shape=jax.ShapeDtypeStruct((bsz, t, RWKV_WIDTH), F32),
        scratch_shapes=[pltpu.VMEM((RWKV_WIDTH // QUAD, QUAD, QUAD), F32)],
        compiler_params=_params(("parallel", "arbitrary")),
        name="rwkv_scan_bwd" if reverse else "rwkv_scan_fwd",
    )(r, v, kk, lin, lex, k, b)


def _na_kernel(q_ref, k0_ref, k1_ref, k2_ref, v0_ref, v1_ref, v2_ref, bias_ref, o_ref):
    k_refs = (k0_ref, k1_ref, k2_ref)
    v_refs = (v0_ref, v1_ref, v2_ref)
    lane_h = lax.broadcasted_iota(jnp.int32, (1, 128), 1) >> 6
    scale = HEAD_DIM ** -0.5
    for p in range(N_HEADS // 2):
        lanes = slice(p * 128, (p + 1) * 128)
        qp = q_ref[:, lanes] * scale
        kp = [kr[:, lanes] for kr in k_refs]
        vp = [vr[:, lanes] for vr in v_refs]
        out_p = None
        for hh in range(2):
            head = lane_h == hh
            qh = jnp.where(head, qp, 0).astype(BF16)
            s = jnp.concatenate([_dot_nt(qh, kj) for kj in kp], axis=1)
            s = s + bias_ref[2 * p + hh].astype(F32)
            m = jnp.max(s, axis=-1, keepdims=True)
            e = jnp.exp(s - m)
            l = jnp.sum(e, axis=-1, keepdims=True)
            eb = e.astype(BF16)
            blk = eb.shape[1] // 3
            pv = (_dot(eb[:, 0:blk], vp[0]) + _dot(eb[:, blk:2 * blk], vp[1])
                  + _dot(eb[:, 2 * blk:3 * blk], vp[2]))
            oh = pv / l
            out_p = oh if out_p is None else jnp.where(head, oh, out_p)
        o_ref[:, lanes] = out_p.astype(o_ref.dtype)


def _na_attn(qkv3, bias):
    b, t, _ = qkv3.shape
    tq = NA_QROWS * GRID_W
    nblk = t // tq

    def kv_spec(col, j):
        return pl.BlockSpec((None, tq, NA_WIDTH),
                            lambda bi, i: (bi, jnp.clip(i - 1, 0, nblk - 3) + j, col))

    variant = lambda i: jnp.where(i == 0, 0, jnp.where(i == nblk - 1, 2, 1))
    return pl.pallas_call(
        _na_kernel,
        grid=(b, nblk),
        in_specs=[pl.BlockSpec((None, tq, NA_WIDTH), lambda bi, i: (bi, i, 0)),
                  kv_spec(1, 0), kv_spec(1, 1), kv_spec(1, 2),
                  kv_spec(2, 0), kv_spec(2, 1), kv_spec(2, 2),
                  pl.BlockSpec((None, N_HEADS, tq, NA_KROWS * GRID_W),
                               lambda bi, i: (variant(i), 0, 0, 0))],
        out_specs=pl.BlockSpec((None, tq, NA_WIDTH), lambda bi, i: (bi, i, 0)),
        out_shape=jax.ShapeDtypeStruct((b, t, NA_WIDTH), BF16),
        compiler_params=_params(("parallel", "arbitrary")),
        name="na_attn",
    )(qkv3, qkv3, qkv3, qkv3, qkv3, qkv3, qkv3, bias)


def _na_bias(rpb):
    j = np.arange(NA_QROWS)[:, None, None, None]
    c = np.arange(GRID_W)[None, :, None, None]
    kr = np.arange(NA_KROWS)[None, None, :, None]
    kc = np.arange(GRID_W)[None, None, None, :]
    cs = np.clip(c - NA_COLS // 2, 0, GRID_W - NA_COLS)
    col_ok = (kc >= cs) & (kc < cs + NA_COLS)
    dc = np.clip(kc - c + NA_COLS - 1, 0, 2 * NA_COLS - 2)
    out = []
    for var in range(3):
        ws = (0 * j, j, 0 * j + 4)[var]
        qrel = (j, j + 4, j + 8)[var]
        ok = col_ok & (kr >= ws) & (kr < ws + NA_ROWS)
        dr = np.clip(kr - qrel + NA_ROWS - 1, 0, 2 * NA_ROWS - 2)
        shape = (NA_QROWS * GRID_W, NA_KROWS * GRID_W)
        full = np.broadcast_to
        dr_i = full(dr, ok.shape).reshape(shape)
        dc_i = full(dc, ok.shape).reshape(shape)
        vals = rpb.astype(F32)[:, dr_i, dc_i]
        out.append(jnp.where(jnp.asarray(ok.reshape(shape))[None], vals, NEG_BIG))
    return jnp.stack(out).astype(BF16)


def _merge_kernel(of_ref, ob_ref, bonus_ref, g_ref, un_ref, gate_ref, x_ref, lnw_ref, lnb_ref,
                  segm_ref, wa_ref, wn_ref, wo_ref, o_ref):
    o = of_ref[...] + ob_ref[...]
    segm = segm_ref[...]
    d = o - _split_dot(o, segm)
    var = _split_dot(d * d, segm)
    on = d * lax.rsqrt(var + GN_EPS)
    u_a = (on * lnw_ref[...] + lnb_ref[...] + bonus_ref[...]) * g_ref[...]
    gate = gate_ref[...]
    m = (gate[:, :D_MODEL] * _dot(u_a.astype(BF16), wa_ref[...])
         + gate[:, D_MODEL:] * _dot(un_ref[...], wn_ref[...]))
    o_ref[...] = x_ref[...] + _dot(m.astype(BF16), wo_ref[...])


def _merge(o_f, o_b, bonus, g, u_n, gates, x2, lnx_w, lnx_b, segm, w_a, w_n, w_o):
    n = x2.shape[0]
    tm = TOKEN_TILE
    row = lambda w: pl.BlockSpec((tm, w), lambda i: (i, 0))
    return pl.pallas_call(
        _merge_kernel,
        grid=(n // tm,),
        in_specs=[row(RWKV_WIDTH)] * 4 + [row(NA_WIDTH), row(GATE_IN), row(D_MODEL),
                  _const_spec((1, RWKV_WIDTH)), _const_spec((1, RWKV_WIDTH)),
                  _const_spec((RWKV_WIDTH, RWKV_WIDTH)), _const_spec((RWKV_WIDTH, D_MODEL)),
                  _const_spec((NA_WIDTH, D_MODEL)), _const_spec((D_MODEL, D_MODEL))],
        out_specs=row(D_MODEL),
        out_shape=jax.ShapeDtypeStruct((n, D_MODEL), F32),
        compiler_params=_params(("parallel",)),
        name="merge",
    )(o_f, o_b, bonus, g, u_n, gates, x2, lnx_w, lnx_b, segm, w_a, w_n, w_o)


def _ffn_tail_kernel(x_ref, p_ref, gf_ref, gp_ref, gl_ref, wg_ref, wu_ref, wd_ref, wple_ref, wpg_ref,
                     o_ref):
    x = x_ref[...]
    h = _rms(x, gf_ref[...]).astype(BF16)
    gt = _dot(h, wg_ref[...])
    act = gt * _sigmoid(gt) * _dot(h, wu_ref[...])
    x = x + _dot(act.astype(BF16), wd_ref[...])
    pg = _sigmoid(_dot(_rms(x, gp_ref[...]).astype(BF16), wpg_ref[...]))
    x = x + _dot(p_ref[...].astype(BF16), wple_ref[...]) * pg
    o_ref[...] = _rms(x, gl_ref[...])


def _ffn_tail(x2, p2, g_ffn, g_ple, g_final, w_gate, w_up, w_down, w_ple, w_pg):
    n = x2.shape[0]
    tm = TOKEN_TILE
    row = lambda w: pl.BlockSpec((tm, w), lambda i: (i, 0))
    vec = _const_spec((1, D_MODEL))
    return pl.pallas_call(
        _ffn_tail_kernel,
        grid=(n // tm,),
        in_specs=[row(D_MODEL), row(PLE_DIM), vec, vec, vec,
                  _const_spec((D_MODEL, D_FF)), _const_spec((D_MODEL, D_FF)),
                  _const_spec((D_FF, D_MODEL)), _const_spec((PLE_DIM, D_MODEL)),
                  _const_spec((D_MODEL, D_MODEL))],
        out_specs=row(D_MODEL),
        out_shape=jax.ShapeDtypeStruct((n, D_MODEL), F32),
        compiler_params=_params(("parallel",)),
        name="ffn_tail",
    )(x2, p2, g_ffn, g_ple, g_final, w_gate, w_up, w_down, w_ple, w_pg)


def _chunk_cumsum_mats(tm):
    t = np.arange(tm)
    same = (t[:, None] // CHUNK) == (t[None, :] // CHUNK)
    fwd = same & (t[None, :] <= t[:, None])
    bwd = same & (t[None, :] >= t[:, None])
    return jnp.asarray(fwd, BF16), jnp.asarray(bwd, BF16)


def _lora_pair(m_f, m_b):
    z = jnp.zeros_like(m_f)
    return jnp.concatenate([jnp.concatenate([m_f, z], axis=1),
                            jnp.concatenate([z, m_b], axis=1)], axis=0).astype(BF16)


def _prepare_weights(g_mix, w_in, mu_prev, mu_next, w0_f, w2_f, w0_b, w2_b, a0_f, a2_f, a0_b, a2_b,
                     g2, k_k, k_a, r_k, lnx_w, lnx_b, rpb, w_br_a, w_br_n, w_out, g_ffn, w_gate,
                     w_up, w_down, g_ple, w_ple, w_pg, g_final):
    w_in = w_in[0].astype(BF16)
    head = np.arange(RWKV_WIDTH) // HEAD_DIM
    seg = (head[:, None] == head[None, :]).astype(np.float32)
    cum_f, cum_b = _chunk_cumsum_mats(TOKEN_TILE)
    vecs = jnp.stack([w0_f[0], w0_b[0], a0_f[0], a0_b[0], k_k[0], k_a[0], r_k[0].reshape(-1),
                      jnp.zeros((RWKV_WIDTH,), F32)])
    return dict(
        g_mix=g_mix[0][None], w_r=w_in[:, :RWKV_IN], w_n=w_in[:, RWKV_IN:RWKV_IN + NA_IN],
        w_g=w_in[:, RWKV_IN + NA_IN:], mu_p=mu_prev[0][None], mu_n=mu_next[0][None], vecs=vecs,
        wlora=_lora_pair(w2_f[0], w2_b[0]), alora=_lora_pair(a2_f[0], a2_b[0]),
        g2=g2[0].astype(BF16), seg=jnp.asarray(seg, BF16), segm=jnp.asarray(seg / HEAD_DIM, BF16),
        cum_f=cum_f, cum_b=cum_b, bias=_na_bias(rpb[0]),
        lnx_w=lnx_w[0][None], lnx_b=lnx_b[0][None], w_a=w_br_a[0].astype(BF16),
        w_nb=w_br_n[0].astype(BF16), w_o=w_out[0].astype(BF16), g_ffn=g_ffn[0][None],
        g_ple=g_ple[0][None], g_final=g_final[None], w_gate=w_gate[0].astype(BF16),
        w_up=w_up[0].astype(BF16), w_down=w_down[0].astype(BF16), w_ple=w_ple[0].astype(BF16),
        w_pg=w_pg[0].astype(BF16))


def _trunk(x, p, w):
    b, t, _ = x.shape
    n = b * t
    x2 = x.reshape(n, D_MODEL)
    z_r, qkv, gates = _in_proj(x2, w["g_mix"], w["w_r"], w["w_n"], w["w_g"])
    (r, v, kk, lin_f, lex_f, k_f, b_f, lin_b, lex_b, k_b, b_b, bonus, g) = _rwkv_prep(
        z_r.reshape(b, t, RWKV_IN), w["mu_p"], w["mu_n"], w["vecs"], w["wlora"], w["alora"], w["g2"],
        w["seg"], w["cum_f"], w["cum_b"])
    o_f = _rwkv_scan(r, v, kk, lin_f, lex_f, k_f, b_f, reverse=False)
    o_b = _rwkv_scan(r, v, kk, lin_b, lex_b, k_b, b_b, reverse=True)
    u_n = _na_attn(qkv.reshape(b, t, NA_IN), w["bias"])
    flat = lambda a: a.reshape(n, a.shape[-1])
    x1 = _merge(flat(o_f), flat(o_b), flat(bonus), flat(g), flat(u_n), gates, x2, w["lnx_w"], w["lnx_b"],
                w["segm"], w["w_a"], w["w_nb"], w["w_o"])
    y = _ffn_tail(x1, p.reshape(n, PLE_DIM), w["g_ffn"], w["g_ple"], w["g_final"], w["w_gate"],
                  w["w_up"], w["w_down"], w["w_ple"], w["w_pg"])
    return y.reshape(b, t, D_MODEL)


def kernel(x_prompt, x_sample, p_prompt, p_sample, g_mix, w_in, mu_prev, mu_next, w0_f, w2_f, w0_b, w2_b, a0_f, a2_f, a0_b, a2_b, g2, k_k, k_a, r_k, lnx_w, lnx_b, rpb, w_br_a, w_br_n, w_out, g_ffn, w_gate, w_up, w_down, g_ple, w_ple, w_pg, g_final):
    w = _prepare_weights(g_mix, w_in, mu_prev, mu_next, w0_f, w2_f, w0_b, w2_b, a0_f, a2_f, a0_b, a2_b,
                         g2, k_k, k_a, r_k, lnx_w, lnx_b, rpb, w_br_a, w_br_n, w_out, g_ffn, w_gate,
                         w_up, w_down, g_ple, w_ple, w_pg, g_final)
    return (_trunk(x_prompt, p_prompt[0], w), _trunk(x_sample, p_sample[0], w))
```

```python
import functools

import numpy as np
import jax
import jax.numpy as jnp
from jax import lax
from jax.experimental import pallas as pl
from jax.experimental.pallas import tpu as pltpu

F32 = jnp.float32
BF16 = jnp.bfloat16

D_MODEL = 1024
HEAD_DIM = 64
RWKV_WIDTH = 512
NA_WIDTH = 512
N_HEADS = 8
RWKV_IN = 1920
NA_IN = 1536
GATE_IN = 2048
D_FF = 2816
PLE_DIM = 256
GRID_W = 64
NA_ROWS = 8
NA_COLS = 16
NORM_EPS = 1e-6
GN_EPS = 64e-5

CHUNK = 64
QUAD = 4 * HEAD_DIM
TOKEN_TILE = 256
SCAN_BLOCK = 512
NA_QROWS = 4
NA_KROWS = 12
NEG_BIG = -1e30
VMEM_LIMIT = 56 * 1024 * 1024


def _dot(a, b):
    return jnp.dot(a, b, preferred_element_type=F32)


def _dot_nt(a, b):
    return lax.dot_general(a, b, (((1,), (1,)), ((), ())), preferred_element_type=F32)


def _dot_tn(a, b):
    return lax.dot_general(a, b, (((0,), (0,)), ((), ())), preferred_element_type=F32)


def _split_dot(x, m):
    hi = x.astype(BF16)
    lo = (x - hi.astype(F32)).astype(BF16)
    return _dot(hi, m) + _dot(lo, m)


def _split3_dot(m, x):
    hi = x.astype(BF16)
    r1 = x - hi.astype(F32)
    mid = r1.astype(BF16)
    lo = (r1 - mid.astype(F32)).astype(BF16)
    return _dot(m, hi) + _dot(m, mid) + _dot(m, lo)


def _sigmoid(x):
    return 1.0 / (1.0 + jnp.exp(-x))


def _rms(x, g):
    ms = jnp.mean(x * x, axis=-1, keepdims=True)
    return x * lax.rsqrt(ms + NORM_EPS) * g


def _const_spec(shape):
    nd = len(shape)
    return pl.BlockSpec(shape, lambda *_: (0,) * nd)


def _params(sem):
    return pltpu.CompilerParams(dimension_semantics=sem, vmem_limit_bytes=VMEM_LIMIT)


def _in_proj_kernel(x_ref, g_ref, wr_ref, wn_ref, wg_ref, zr_ref, qkv_ref, gate_ref):
    h = _rms(x_ref[...], g_ref[...]).astype(BF16)
    zr_ref[...] = _dot(h, wr_ref[...])
    qkv_ref[...] = _dot(h, wn_ref[...]).astype(BF16)
    gate_ref[...] = _sigmoid(_dot(h, wg_ref[...])).astype(BF16)


def _in_proj(x2, g_mix, w_r, w_n, w_g):
    n = x2.shape[0]
    tm = TOKEN_TILE
    row = lambda w: pl.BlockSpec((tm, w), lambda i: (i, 0))
    return pl.pallas_call(
        _in_proj_kernel,
        grid=(n // tm,),
        in_specs=[row(D_MODEL), _const_spec((1, D_MODEL)), _const_spec((D_MODEL, RWKV_IN)),
                  _const_spec((D_MODEL, NA_IN)), _const_spec((D_MODEL, GATE_IN))],
        out_specs=[row(RWKV_IN), row(NA_IN), row(GATE_IN)],
        out_shape=[jax.ShapeDtypeStruct((n, RWKV_IN), F32),
                   jax.ShapeDtypeStruct((n, NA_IN), BF16),
                   jax.ShapeDtypeStruct((n, GATE_IN), BF16)],
        compiler_params=_params(("parallel",)),
        name="in_proj",
    )(x2, g_mix, w_r, w_n, w_g)


def _rwkv_prep_kernel(z_ref, zp_ref, zn_ref, mup_ref, mun_ref, vec_ref, wlora_ref, alora_ref,
                      g2_ref, seg_ref, cumf_ref, cumb_ref,
                      r_ref, v_ref, kk_ref, linf_ref, lexf_ref, kf_ref, bf_ref,
                      linb_ref, lexb_ref, kb_ref, bb_ref, bonus_ref, g_ref):
    i = pl.program_id(1)
    n_i = pl.num_programs(1)
    tm = z_ref.shape[0]
    z = z_ref[...]
    prev_row = jnp.where(i > 0, zp_ref[7:8, :], 0.0)
    next_row = jnp.where(i < n_i - 1, zn_ref[0:1, :], 0.0)
    rows = lax.broadcasted_iota(jnp.int32, (tm, 1), 0)
    zp = jnp.where(rows == 0, prev_row, pltpu.roll(z, 1, 0))
    zn = jnp.where(rows == tm - 1, next_row, pltpu.roll(z, tm - 1, 0))
    zs = z + mup_ref[...] * (zp - z) + mun_ref[...] * (zn - z)

    w = RWKV_WIDTH
    r, k, v = zs[:, 0:w], zs[:, w:2 * w], zs[:, 2 * w:3 * w]
    wd = zs[:, 3 * w:3 * w + 128]
    ad = zs[:, 3 * w + 128:3 * w + 256]
    gd = zs[:, 3 * w + 256:3 * w + 384]

    w0_f, w0_b = vec_ref[0:1, :], vec_ref[1:2, :]
    a0_f, a0_b = vec_ref[2:3, :], vec_ref[3:4, :]
    k_k, k_a, r_k = vec_ref[4:5, :], vec_ref[5:6, :], vec_ref[6:7, :]

    wl = _dot(jnp.tanh(wd).astype(BF16), wlora_ref[...])
    al = _dot(ad.astype(BF16), alora_ref[...])

    def log_decay(w0, lora):
        y = -(w0 + lora)
        softplus = jnp.maximum(y, 0.0) + jnp.log(1.0 + jnp.exp(-jnp.abs(y)))
        return -jnp.exp(-softplus - 0.5)

    lw_f = log_decay(w0_f, wl[:, 0:w])
    lw_b = log_decay(w0_b, wl[:, w:2 * w])
    a_f = _sigmoid(a0_f + al[:, 0:w])
    a_b = _sigmoid(a0_b + al[:, w:2 * w])
    g_ref[...] = _dot(_sigmoid(gd).astype(BF16), g2_ref[...]).astype(g_ref.dtype)

    k_f = k * (1.0 + (a_f - 1.0) * k_a)
    k_b = k * (1.0 + (a_b - 1.0) * k_a)
    kk = k * k_k
    seg = seg_ref[...]
    ss = _split_dot(kk * kk, seg)
    kk = kk / jnp.maximum(jnp.sqrt(ss), 1e-12)
    bonus_ref[...] = (_split_dot(r * (0.5 * (k_f + k_b)) * r_k, seg) * v).astype(bonus_ref.dtype)

    lin_f = _split3_dot(cumf_ref[...], lw_f)
    lin_b = _split3_dot(cumb_ref[...], lw_b)
    r_ref[...] = r.astype(r_ref.dtype)
    v_ref[...] = v.astype(v_ref.dtype)
    kk_ref[...] = kk.astype(kk_ref.dtype)
    linf_ref[...] = lin_f
    lexf_ref[...] = lin_f - lw_f
    kf_ref[...] = k_f.astype(kf_ref.dtype)
    bf_ref[...] = (kk * a_f).astype(bf_ref.dtype)
    linb_ref[...] = lin_b
    lexb_ref[...] = lin_b - lw_b
    kb_ref[...] = k_b.astype(kb_ref.dtype)
    bb_ref[...] = (kk * a_b).astype(bb_ref.dtype)


def _rwkv_prep(z3, mu_p, mu_n, vecs, wlora, alora, g2, seg, cum_f, cum_b):
    b, t, _ = z3.shape
    tm = TOKEN_TILE
    nb8 = tm // 8
    tile = lambda w: pl.BlockSpec((None, tm, w), lambda bi, i: (bi, i, 0))
    halo_p = pl.BlockSpec((None, 8, RWKV_IN), lambda bi, i: (bi, jnp.maximum(i * nb8 - 1, 0), 0))
    halo_n = pl.BlockSpec((None, 8, RWKV_IN),
                          lambda bi, i: (bi, jnp.minimum((i + 1) * nb8, t // 8 - 1), 0))
    out = lambda dt: jax.ShapeDtypeStruct((b, t, RWKV_WIDTH), dt)
    dtypes = [BF16] * 3 + [F32] * 2 + [BF16] * 2 + [F32] * 2 + [BF16] * 4
    return pl.pallas_call(
        _rwkv_prep_kernel,
        grid=(b, t // tm),
        in_specs=[tile(RWKV_IN), halo_p, halo_n, _const_spec((1, RWKV_IN)), _const_spec((1, RWKV_IN)),
                  _const_spec((8, RWKV_WIDTH)), _const_spec((128, 2 * RWKV_WIDTH)),
                  _const_spec((128, 2 * RWKV_WIDTH)), _const_spec((128, RWKV_WIDTH)),
                  _const_spec((RWKV_WIDTH, RWKV_WIDTH)), _const_spec((tm, tm)), _const_spec((tm, tm))],
        out_specs=[tile(RWKV_WIDTH)] * 13,
        out_shape=[out(dt) for dt in dtypes],
        compiler_params=_params(("parallel", "parallel")),
        name="rwkv_prep",
    )(z3, z3, z3, mu_p, mu_n, vecs, wlora, alora, g2, seg, cum_f, cum_b)


def _block_diag(x, same_head):
    tiled = jnp.concatenate([x, x, x, x], axis=0)
    return jnp.where(same_head, tiled, 0.0).astype(BF16)


def _rwkv_scan_kernel(r_ref, v_ref, kk_ref, lin_ref, lex_ref, k_ref, b_ref, o_ref, s_ref, *, reverse):
    n_chunks = r_ref.shape[0] // CHUNK
    n_quads = RWKV_WIDTH // QUAD

    @pl.when(pl.program_id(1) == 0)
    def _():
        s_ref[...] = jnp.zeros_like(s_ref)

    row_h = lax.broadcasted_iota(jnp.int32, (QUAD, QUAD), 0) >> 6
    col_h = lax.broadcasted_iota(jnp.int32, (QUAD, QUAD), 1) >> 6
    same_head = row_h == col_h
    t_idx = lax.broadcasted_iota(jnp.int32, (CHUNK, QUAD), 0)
    s_idx = lax.broadcasted_iota(jnp.int32, (CHUNK, QUAD), 1) & (CHUNK - 1)
    if reverse:
        strict = s_idx > t_idx
        out_mask = strict
    else:
        strict = s_idx < t_idx
        out_mask = s_idx <= t_idx
    eye = (s_idx == t_idx).astype(F32)

    order = range(n_chunks - 1, -1, -1) if reverse else range(n_chunks)
    units = [(c, q) for c in order for q in range(n_quads)]
    n_sq = CHUNK.bit_length() - 1

    st = {}
    for c, q in units:
        rows = slice(c * CHUNK, (c + 1) * CHUNK)
        lanes = slice(q * QUAD, (q + 1) * QUAD)
        tot_row = c * CHUNK if reverse else (c + 1) * CHUNK - 1
        lin = lin_ref[rows, lanes]
        tot = lin_ref[tot_row:tot_row + 1, lanes]
        e_ex = jnp.exp(lex_ref[rows, lanes])
        e_neg = jnp.exp(-lin)
        e_end = jnp.exp(tot - lin)
        v = v_ref[rows, lanes].astype(F32)
        k = k_ref[rows, lanes].astype(F32)
        b = b_ref[rows, lanes].astype(F32)
        a_t = -kk_ref[rows, lanes].astype(F32) * e_ex
        r_t = r_ref[rows, lanes].astype(F32) * (e_ex if reverse else jnp.exp(lin))
        st[c, q] = dict(
            a_t=a_t, r_t=r_t, v=v, decay=jnp.exp(tot),
            ar=jnp.concatenate([a_t, r_t], axis=0).astype(BF16),
            bt_bd=_block_diag(b * e_neg, same_head), kt_bd=_block_diag(k * e_neg, same_head),
            v_bd=_block_diag(v, same_head),
            bk=jnp.concatenate([b * e_end, k * e_end], axis=0).astype(BF16))
    for u in units:
        s = st[u]
        sb = _dot_nt(s["ar"], s["bt_bd"])
        sk = _dot_nt(s["ar"], s["kt_bd"])
        a_ab = jnp.where(strict, sb[:CHUNK], 0.0)
        s.update(p=a_ab, t_inv=eye + a_ab, m_rb=jnp.where(out_mask, sb[CHUNK:], 0.0).astype(BF16),
                 akm=jnp.concatenate([jnp.where(strict, sk[:CHUNK], 0.0),
                                      jnp.where(out_mask, sk[CHUNK:], 0.0)], axis=0).astype(BF16))
    for u in units:
        s = st[u]
        kv = _dot(s["akm"], s["v_bd"])
        s["akv"] = kv[:CHUNK]
        s["y0"] = kv[CHUNK:]

    for j in range(n_sq - 1):
        for u in units:
            s = st[u]
            bd = _block_diag(s["p"], same_head)
            if j == 0:
                s["p"] = _dot(s["p"].astype(BF16), bd)
            else:
                tp = _dot(jnp.concatenate([s["t_inv"], s["p"]], axis=0).astype(BF16), bd)
                s["t_inv"] = s["t_inv"] + tp[:CHUNK]
                s["p"] = tp[CHUNK:]
    for u in units:
        s = st[u]
        t_inv = s["t_inv"] + _dot(s["t_inv"].astype(BF16), _block_diag(s["p"], same_head))
        s["t_b"] = t_inv.astype(BF16)
    for u in units:
        s = st[u]
        w_mat = _dot(s["t_b"], _block_diag(s["a_t"], same_head))
        s["u0"] = _dot(s["t_b"], _block_diag(s["akv"], same_head))
        s["wr"] = jnp.concatenate([w_mat, s["r_t"]], axis=0).astype(BF16)
    for u in units:
        s = st[u]
        s["g_mat"] = jnp.where(same_head, _dot_tn(s["wr"][:CHUNK], s["bk"][:CHUNK]), 0.0).astype(BF16)
        uv0 = jnp.concatenate([s["u0"], s["v"]], axis=0).astype(BF16)
        s["j_mat"] = jnp.where(same_head, _dot_tn(uv0, s["bk"]), 0.0)

    state = [s_ref[q] for q in range(n_quads)]

    def emit_y(c):
        for q in range(n_quads):
            s = st[c, q]
            y = s["y1"] + _dot(s["m_rb"], _block_diag(s["u_mat"], same_head))
            o_ref[c * CHUNK:(c + 1) * CHUNK, q * QUAD:(q + 1) * QUAD] = y.astype(o_ref.dtype)

    prev = None
    for c in order:
        for q in range(n_quads):
            s = st[c, q]
            s["s_in"] = state[q].astype(BF16)
            state[q] = state[q] * s["decay"] + _dot(s["s_in"], s["g_mat"]) + s["j_mat"]
        for q in range(n_quads):
            s = st[c, q]
            ws = _dot_nt(s["wr"], s["s_in"])
            s["u_mat"] = ws[:CHUNK] + s["u0"]
            s["y1"] = ws[CHUNK:] + s["y0"]
        if prev is not None:
            emit_y(prev)
        prev = c
    emit_y(prev)
    for q in range(n_quads):
        s_ref[q] = state[q]


def _rwkv_scan(r, v, kk, lin, lex, k, b, *, reverse):
    bsz, t, _ = r.shape
    tb = SCAN_BLOCK
    nblk = t // tb
    if reverse:
        idx = lambda bi, i: (bi, nblk - 1 - i, 0)
    else:
        idx = lambda bi, i: (bi, i, 0)
    spec = pl.BlockSpec((None, tb, RWKV_WIDTH), idx)
    return pl.pallas_call(
        functools.partial(_rwkv_scan_kernel, reverse=reverse),
        grid=(bsz, nblk),
        in_specs=[spec] * 7,
        out_specs=spec,
        out_shape=jax.ShapeDtypeStruct((bsz, t, RWKV_WIDTH), BF16),
        scratch_shapes=[pltpu.VMEM((RWKV_WIDTH // QUAD, QUAD, QUAD), F32)],
        compiler_params=_params(("parallel", "arbitrary")),
        name="rwkv_scan_bwd" if reverse else "rwkv_scan_fwd",
    )(r, v, kk, lin, lex, k, b)


def _na_kernel(q_ref, k0_ref, k1_ref, k2_ref, v0_ref, v1_ref, v2_ref, bias_ref, o_ref):
    k_refs = (k0_ref, k1_ref, k2_ref)
    v_refs = (v0_ref, v1_ref, v2_ref)
    lane_h = lax.broadcasted_iota(jnp.int32, (1, 128), 1) >> 6
    scale = HEAD_DIM ** -0.5
    for p in range(N_HEADS // 2):
        lanes = slice(p * 128, (p + 1) * 128)
        qp = q_ref[:, lanes] * scale
        kp = [kr[:, lanes] for kr in k_refs]
        vp = [vr[:, lanes] for vr in v_refs]
        out_p = None
        for hh in range(2):
            head = lane_h == hh
            qh = jnp.where(head, qp, 0).astype(BF16)
            s = jnp.concatenate([_dot_nt(qh, kj) for kj in kp], axis=1)
            s = s + bias_ref[2 * p + hh].astype(F32)
            m = jnp.max(s, axis=-1, keepdims=True)
            e = jnp.exp(s - m)
            l = jnp.sum(e, axis=-1, keepdims=True)
            eb = e.astype(BF16)
            blk = eb.shape[1] // 3
            pv = (_dot(eb[:, 0:blk], vp[0]) + _dot(eb[:, blk:2 * blk], vp[1])
                  + _dot(eb[:, 2 * blk:3 * blk], vp[2]))
            oh = pv / l
            out_p = oh if out_p is None else jnp.where(head, oh, out_p)
        o_ref[:, lanes] = out_p.astype(o_ref.dtype)


def _na_attn(qkv3, bias):
    b, t, _ = qkv3.shape
    tq = NA_QROWS * GRID_W
    nblk = t // tq

    def kv_spec(col, j):
        return pl.BlockSpec((None, tq, NA_WIDTH),
                            lambda bi, i: (bi, jnp.clip(i - 1, 0, nblk - 3) + j, col))

    variant = lambda i: jnp.where(i == 0, 0, jnp.where(i == nblk - 1, 2, 1))
    return pl.pallas_call(
        _na_kernel,
        grid=(b, nblk),
        in_specs=[pl.BlockSpec((None, tq, NA_WIDTH), lambda bi, i: (bi, i, 0)),
                  kv_spec(1, 0), kv_spec(1, 1), kv_spec(1, 2),
                  kv_spec(2, 0), kv_spec(2, 1), kv_spec(2, 2),
                  pl.BlockSpec((None, N_HEADS, tq, NA_KROWS * GRID_W),
                               lambda bi, i: (variant(i), 0, 0, 0))],
        out_specs=pl.BlockSpec((None, tq, NA_WIDTH), lambda bi, i: (bi, i, 0)),
        out_shape=jax.ShapeDtypeStruct((b, t, NA_WIDTH), BF16),
        compiler_params=_params(("parallel", "arbitrary")),
        name="na_attn",
    )(qkv3, qkv3, qkv3, qkv3, qkv3, qkv3, qkv3, bias)


def _na_bias(rpb):
    j = np.arange(NA_QROWS)[:, None]
    kr = np.arange(NA_KROWS)[None, :]
    c = np.arange(GRID_W)[:, None]
    kc = np.arange(GRID_W)[None, :]
    cs = np.clip(c - NA_COLS // 2, 0, GRID_W - NA_COLS)
    col_ok = (kc >= cs) & (kc < cs + NA_COLS)
    dc = np.clip(kc - c + NA_COLS - 1, 0, 2 * NA_COLS - 2)
    col_sel = (dc[..., None] == np.arange(2 * NA_COLS - 1)) & col_ok[..., None]
    row_sel, masks = [], []
    for var in range(3):
        ws = (0 * j, j, 0 * j + 4)[var]
        qrel = (j, j + 4, j + 8)[var]
        row_ok = (kr >= ws) & (kr < ws + NA_ROWS)
        dr = np.clip(kr - qrel + NA_ROWS - 1, 0, 2 * NA_ROWS - 2)
        row_sel.append((dr[..., None] == np.arange(2 * NA_ROWS - 1)) & row_ok[..., None])
        masks.append(row_ok[:, None, :, None] & col_ok[None, :, None, :])
    row_sel = jnp.asarray(np.stack(row_sel), F32)
    mask = jnp.asarray(np.stack(masks))[:, None]
    vals = jnp.einsum("vjra,hab,ckb->vhjcrk", row_sel, rpb.astype(F32), jnp.asarray(col_sel, F32),
                      precision=lax.Precision.HIGHEST)
    bias = jnp.where(mask, vals, NEG_BIG).astype(BF16)
    return bias.reshape(3, N_HEADS, NA_QROWS * GRID_W, NA_KROWS * GRID_W)


def _merge_kernel(of_ref, ob_ref, bonus_ref, g_ref, un_ref, gate_ref, x_ref, lnw_ref, lnb_ref,
                  segm_ref, wa_ref, wn_ref, wo_ref, o_ref):
    o = of_ref[...].astype(F32) + ob_ref[...].astype(F32)
    segm = segm_ref[...]
    d = o - _split_dot(o, segm)
    var = _split_dot(d * d, segm)
    on = d * lax.rsqrt(var + GN_EPS)
    u_a = (on * lnw_ref[...] + lnb_ref[...] + bonus_ref[...]) * g_ref[...]
    gate = gate_ref[...]
    m = (gate[:, :D_MODEL] * _dot(u_a.astype(BF16), wa_ref[...])
         + gate[:, D_MODEL:] * _dot(un_ref[...], wn_ref[...]))
    o_ref[...] = x_ref[...] + _dot(m.astype(BF16), wo_ref[...])


def _merge(o_f, o_b, bonus, g, u_n, gates, x2, lnx_w, lnx_b, segm, w_a, w_n, w_o):
    n = x2.shape[0]
    tm = TOKEN_TILE
    row = lambda w: pl.BlockSpec((tm, w), lambda i: (i, 0))
    return pl.pallas_call(
        _merge_kernel,
        grid=(n // tm,),
        in_specs=[row(RWKV_WIDTH)] * 4 + [row(NA_WIDTH), row(GATE_IN), row(D_MODEL),
                  _const_spec((1, RWKV_WIDTH)), _const_spec((1, RWKV_WIDTH)),
                  _const_spec((RWKV_WIDTH, RWKV_WIDTH)), _const_spec((RWKV_WIDTH, D_MODEL)),
                  _const_spec((NA_WIDTH, D_MODEL)), _const_spec((D_MODEL, D_MODEL))],
        out_specs=row(D_MODEL),
        out_shape=jax.ShapeDtypeStruct((n, D_MODEL), F32),
        compiler_params=_params(("parallel",)),
        name="merge",
    )(o_f, o_b, bonus, g, u_n, gates, x2, lnx_w, lnx_b, segm, w_a, w_n, w_o)


def _ffn_tail_kernel(x_ref, p_ref, gf_ref, gp_ref, gl_ref, wg_ref, wu_ref, wd_ref, wple_ref, wpg_ref,
                     o_ref):
    x = x_ref[...]
    h = _rms(x, gf_ref[...]).astype(BF16)
    gt = _dot(h, wg_ref[...])
    act = gt * _sigmoid(gt) * _dot(h, wu_ref[...])
    x = x + _dot(act.astype(BF16), wd_ref[...])
    pg = _sigmoid(_dot(_rms(x, gp_ref[...]).astype(BF16), wpg_ref[...]))
    x = x + _dot(p_ref[...].astype(BF16), wple_ref[...]) * pg
    o_ref[...] = _rms(x, gl_ref[...])


def _ffn_tail(x2, p2, g_ffn, g_ple, g_final, w_gate, w_up, w_down, w_ple, w_pg):
    n = x2.shape[0]
    tm = TOKEN_TILE
    row = lambda w: pl.BlockSpec((tm, w), lambda i: (i, 0))
    vec = _const_spec((1, D_MODEL))
    return pl.pallas_call(
        _ffn_tail_kernel,
        grid=(n // tm,),
        in_specs=[row(D_MODEL), row(PLE_DIM), vec, vec, vec,
                  _const_spec((D_MODEL, D_FF)), _const_spec((D_MODEL, D_FF)),
                  _const_spec((D_FF, D_MODEL)), _const_spec((PLE_DIM, D_MODEL)),
                  _const_spec((D_MODEL, D_MODEL))],
        out_specs=row(D_MODEL),
        out_shape=jax.ShapeDtypeStruct((n, D_MODEL), F32),
        compiler_params=_params(("parallel",)),
        name="ffn_tail",
    )(x2, p2, g_ffn, g_ple, g_final, w_gate, w_up, w_down, w_ple, w_pg)


def _chunk_cumsum_mats(tm):
    t = np.arange(tm)
    same = (t[:, None] // CHUNK) == (t[None, :] // CHUNK)
    fwd = same & (t[None, :] <= t[:, None])
    bwd = same & (t[None, :] >= t[:, None])
    return jnp.asarray(fwd, BF16), jnp.asarray(bwd, BF16)


def _lora_pair(m_f, m_b):
    z = jnp.zeros_like(m_f)
    return jnp.concatenate([jnp.concatenate([m_f, z], axis=1),
                            jnp.concatenate([z, m_b], axis=1)], axis=0).astype(BF16)


def _prepare_weights(g_mix, w_in, mu_prev, mu_next, w0_f, w2_f, w0_b, w2_b, a0_f, a2_f, a0_b, a2_b,
                     g2, k_k, k_a, r_k, lnx_w, lnx_b, rpb, w_br_a, w_br_n, w_out, g_ffn, w_gate,
                     w_up, w_down, g_ple, w_ple, w_pg, g_final):
    w_in = w_in[0].astype(BF16)
    head = np.arange(RWKV_WIDTH) // HEAD_DIM
    seg = (head[:, None] == head[None, :]).astype(np.float32)
    cum_f, cum_b = _chunk_cumsum_mats(TOKEN_TILE)
    vecs = jnp.stack([w0_f[0], w0_b[0], a0_f[0], a0_b[0], k_k[0], k_a[0], r_k[0].reshape(-1),
                      jnp.zeros((RWKV_WIDTH,), F32)])
    return dict(
        g_mix=g_mix[0][None], w_r=w_in[:, :RWKV_IN], w_n=w_in[:, RWKV_IN:RWKV_IN + NA_IN],
        w_g=w_in[:, RWKV_IN + NA_IN:], mu_p=mu_prev[0][None], mu_n=mu_next[0][None], vecs=vecs,
        wlora=_lora_pair(w2_f[0], w2_b[0]), alora=_lora_pair(a2_f[0], a2_b[0]),
        g2=g2[0].astype(BF16), seg=jnp.asarray(seg, BF16), segm=jnp.asarray(seg / HEAD_DIM, BF16),
        cum_f=cum_f, cum_b=cum_b, bias=_na_bias(rpb[0]),
        lnx_w=lnx_w[0][None], lnx_b=lnx_b[0][None], w_a=w_br_a[0].astype(BF16),
        w_nb=w_br_n[0].astype(BF16), w_o=w_out[0].astype(BF16), g_ffn=g_ffn[0][None],
        g_ple=g_ple[0][None], g_final=g_final[None], w_gate=w_gate[0].astype(BF16),
        w_up=w_up[0].astype(BF16), w_down=w_down[0].astype(BF16), w_ple=w_ple[0].astype(BF16),
        w_pg=w_pg[0].astype(BF16))


def _trunk(x, p, w):
    b, t, _ = x.shape
    n = b * t
    x2 = x.reshape(n, D_MODEL)
    z_r, qkv, gates = _in_proj(x2, w["g_mix"], w["w_r"], w["w_n"], w["w_g"])
    (r, v, kk, lin_f, lex_f, k_f, b_f, lin_b, lex_b, k_b, b_b, bonus, g) = _rwkv_prep(
        z_r.reshape(b, t, RWKV_IN), w["mu_p"], w["mu_n"], w["vecs"], w["wlora"], w["alora"], w["g2"],
        w["seg"], w["cum_f"], w["cum_b"])
    o_f = _rwkv_scan(r, v, kk, lin_f, lex_f, k_f, b_f, reverse=False)
    o_b = _rwkv_scan(r, v, kk, lin_b, lex_b, k_b, b_b, reverse=True)
    u_n = _na_attn(qkv.reshape(b, t, NA_IN), w["bias"])
    flat = lambda a: a.reshape(n, a.shape[-1])
    x1 = _merge(flat(o_f), flat(o_b), flat(bonus), flat(g), flat(u_n), gates, x2, w["lnx_w"], w["lnx_b"],
                w["segm"], w["w_a"], w["w_nb"], w["w_o"])
    y = _ffn_tail(x1, p.reshape(n, PLE_DIM), w["g_ffn"], w["g_ple"], w["g_final"], w["w_gate"],
                  w["w_up"], w["w_down"], w["w_ple"], w["w_pg"])
    return y.reshape(b, t, D_MODEL)


def kernel(x_prompt, x_sample, p_prompt, p_sample, g_mix, w_in, mu_prev, mu_next, w0_f, w2_f, w0_b, w2_b, a0_f, a2_f, a0_b, a2_b, g2, k_k, k_a, r_k, lnx_w, lnx_b, rpb, w_br_a, w_br_n, w_out, g_ffn, w_gate, w_up, w_down, g_ple, w_ple, w_pg, g_final):
    w = _prepare_weights(g_mix, w_in, mu_prev, mu_next, w0_f, w2_f, w0_b, w2_b, a0_f, a2_f, a0_b, a2_b,
                         g2, k_k, k_a, r_k, lnx_w, lnx_b, rpb, w_br_a, w_br_n, w_out, g_ffn, w_gate,
                         w_up, w_down, g_ple, w_ple, w_pg, g_final)
    return (_trunk(x_prompt, p_prompt[0], w), _trunk(x_sample, p_sample[0], w))
```

```python
import functools

import numpy as np
import jax
import jax.numpy as jnp
from jax import lax
from jax.experimental import pallas as pl
from jax.experimental.pallas import tpu as pltpu

F32 = jnp.float32
BF16 = jnp.bfloat16

D_MODEL = 1024
HEAD_DIM = 64
RWKV_WIDTH = 512
NA_WIDTH = 512
N_HEADS = 8
RWKV_IN = 1920
NA_IN = 1536
GATE_IN = 2048
D_FF = 2816
PLE_DIM = 256
GRID_W = 64
NA_ROWS = 8
NA_COLS = 16
NORM_EPS = 1e-6
GN_EPS = 64e-5

CHUNK = 64
QUAD = 4 * HEAD_DIM
TOKEN_TILE = 256
MATMUL_TILE = 512
SCAN_BLOCK = 512
NA_QROWS = 4
NA_KROWS = 12
NEG_BIG = -1e30
VMEM_LIMIT = 56 * 1024 * 1024


def _dot(a, b):
    return jnp.dot(a, b, preferred_element_type=F32)


def _dot_nt(a, b):
    return lax.dot_general(a, b, (((1,), (1,)), ((), ())), preferred_element_type=F32)


def _dot_tn(a, b):
    return lax.dot_general(a, b, (((0,), (0,)), ((), ())), preferred_element_type=F32)


def _split_dot(x, m):
    hi = x.astype(BF16)
    lo = (x - hi.astype(F32)).astype(BF16)
    return _dot(hi, m) + _dot(lo, m)


def _split3_dot(m, x):
    hi = x.astype(BF16)
    r1 = x - hi.astype(F32)
    mid = r1.astype(BF16)
    lo = (r1 - mid.astype(F32)).astype(BF16)
    return _dot(m, hi) + _dot(m, mid) + _dot(m, lo)


def _sigmoid(x):
    return 1.0 / (1.0 + jnp.exp(-x))


def _rms(x, g):
    ms = jnp.mean(x * x, axis=-1, keepdims=True)
    return x * lax.rsqrt(ms + NORM_EPS) * g


def _const_spec(shape):
    nd = len(shape)
    return pl.BlockSpec(shape, lambda *_: (0,) * nd, pipeline_mode=pl.Buffered(1))


def _params(sem):
    return pltpu.CompilerParams(dimension_semantics=sem, vmem_limit_bytes=VMEM_LIMIT)


def _in_proj_kernel(x_ref, g_ref, wr_ref, wn_ref, wg_ref, zr_ref, qkv_ref, gate_ref):
    h = _rms(x_ref[...], g_ref[...]).astype(BF16)
    zr_ref[...] = _dot(h, wr_ref[...])
    qkv_ref[...] = _dot(h, wn_ref[...]).astype(BF16)
    gate_ref[...] = _sigmoid(_dot(h, wg_ref[...])).astype(BF16)


def _in_proj(x2, g_mix, w_r, w_n, w_g):
    n = x2.shape[0]
    tm = MATMUL_TILE
    row = lambda w: pl.BlockSpec((tm, w), lambda i: (i, 0))
    return pl.pallas_call(
        _in_proj_kernel,
        grid=(n // tm,),
        in_specs=[row(D_MODEL), _const_spec((1, D_MODEL)), _const_spec((D_MODEL, RWKV_IN)),
                  _const_spec((D_MODEL, NA_IN)), _const_spec((D_MODEL, GATE_IN))],
        out_specs=[row(RWKV_IN), row(NA_IN), row(GATE_IN)],
        out_shape=[jax.ShapeDtypeStruct((n, RWKV_IN), F32),
                   jax.ShapeDtypeStruct((n, NA_IN), BF16),
                   jax.ShapeDtypeStruct((n, GATE_IN), BF16)],
        compiler_params=_params(("parallel",)),
        name="in_proj",
    )(x2, g_mix, w_r, w_n, w_g)


def _rwkv_prep_kernel(z_ref, zp_ref, zn_ref, mup_ref, mun_ref, vec_ref, wlora_ref, alora_ref,
                      g2_ref, seg_ref, cumf_ref, cumb_ref,
                      r_ref, v_ref, kk_ref, linf_ref, lexf_ref, kf_ref, bf_ref,
                      linb_ref, lexb_ref, kb_ref, bb_ref, bonus_ref, g_ref):
    i = pl.program_id(1)
    n_i = pl.num_programs(1)
    tm = z_ref.shape[0]
    z = z_ref[...]
    prev_row = jnp.where(i > 0, zp_ref[7:8, :], 0.0)
    next_row = jnp.where(i < n_i - 1, zn_ref[0:1, :], 0.0)
    rows = lax.broadcasted_iota(jnp.int32, (tm, 1), 0)
    zp = jnp.where(rows == 0, prev_row, pltpu.roll(z, 1, 0))
    zn = jnp.where(rows == tm - 1, next_row, pltpu.roll(z, tm - 1, 0))
    zs = z + mup_ref[...] * (zp - z) + mun_ref[...] * (zn - z)

    w = RWKV_WIDTH
    r, k, v = zs[:, 0:w], zs[:, w:2 * w], zs[:, 2 * w:3 * w]
    wd = zs[:, 3 * w:3 * w + 128]
    ad = zs[:, 3 * w + 128:3 * w + 256]
    gd = zs[:, 3 * w + 256:3 * w + 384]

    w0_f, w0_b = vec_ref[0:1, :], vec_ref[1:2, :]
    a0_f, a0_b = vec_ref[2:3, :], vec_ref[3:4, :]
    k_k, k_a, r_k = vec_ref[4:5, :], vec_ref[5:6, :], vec_ref[6:7, :]

    wl = _dot(jnp.tanh(wd).astype(BF16), wlora_ref[...])
    al = _dot(ad.astype(BF16), alora_ref[...])

    def log_decay(w0, lora):
        y = -(w0 + lora)
        softplus = jnp.maximum(y, 0.0) + jnp.log(1.0 + jnp.exp(-jnp.abs(y)))
        return -jnp.exp(-softplus - 0.5)

    lw_f = log_decay(w0_f, wl[:, 0:w])
    lw_b = log_decay(w0_b, wl[:, w:2 * w])
    a_f = _sigmoid(a0_f + al[:, 0:w])
    a_b = _sigmoid(a0_b + al[:, w:2 * w])
    g_ref[...] = _dot(_sigmoid(gd).astype(BF16), g2_ref[...]).astype(g_ref.dtype)

    k_f = k * (1.0 + (a_f - 1.0) * k_a)
    k_b = k * (1.0 + (a_b - 1.0) * k_a)
    kk = k * k_k
    seg = seg_ref[...]
    ss = _split_dot(kk * kk, seg)
    kk = kk / jnp.maximum(jnp.sqrt(ss), 1e-12)
    bonus_ref[...] = (_split_dot(r * (0.5 * (k_f + k_b)) * r_k, seg) * v).astype(bonus_ref.dtype)

    lin_f = _split3_dot(cumf_ref[...], lw_f)
    lin_b = _split3_dot(cumb_ref[...], lw_b)
    r_ref[...] = r.astype(r_ref.dtype)
    v_ref[...] = v.astype(v_ref.dtype)
    kk_ref[...] = kk.astype(kk_ref.dtype)
    linf_ref[...] = lin_f
    lexf_ref[...] = lin_f - lw_f
    kf_ref[...] = k_f.astype(kf_ref.dtype)
    bf_ref[...] = (kk * a_f).astype(bf_ref.dtype)
    linb_ref[...] = lin_b
    lexb_ref[...] = lin_b - lw_b
    kb_ref[...] = k_b.astype(kb_ref.dtype)
    bb_ref[...] = (kk * a_b).astype(bb_ref.dtype)


def _rwkv_prep(z3, mu_p, mu_n, vecs, wlora, alora, g2, seg, cum_f, cum_b):
    b, t, _ = z3.shape
    tm = TOKEN_TILE
    nb8 = tm // 8
    tile = lambda w: pl.BlockSpec((None, tm, w), lambda bi, i: (bi, i, 0))
    halo_p = pl.BlockSpec((None, 8, RWKV_IN), lambda bi, i: (bi, jnp.maximum(i * nb8 - 1, 0), 0))
    halo_n = pl.BlockSpec((None, 8, RWKV_IN),
                          lambda bi, i: (bi, jnp.minimum((i + 1) * nb8, t // 8 - 1), 0))
    out = lambda dt: jax.ShapeDtypeStruct((b, t, RWKV_WIDTH), dt)
    dtypes = [BF16] * 3 + [F32] * 2 + [BF16] * 2 + [F32] * 2 + [BF16] * 4
    return pl.pallas_call(
        _rwkv_prep_kernel,
        grid=(b, t // tm),
        in_specs=[tile(RWKV_IN), halo_p, halo_n, _const_spec((1, RWKV_IN)), _const_spec((1, RWKV_IN)),
                  _const_spec((8, RWKV_WIDTH)), _const_spec((128, 2 * RWKV_WIDTH)),
                  _const_spec((128, 2 * RWKV_WIDTH)), _const_spec((128, RWKV_WIDTH)),
                  _const_spec((RWKV_WIDTH, RWKV_WIDTH)), _const_spec((tm, tm)), _const_spec((tm, tm))],
        out_specs=[tile(RWKV_WIDTH)] * 13,
        out_shape=[out(dt) for dt in dtypes],
        compiler_params=_params(("parallel", "parallel")),
        name="rwkv_prep",
    )(z3, z3, z3, mu_p, mu_n, vecs, wlora, alora, g2, seg, cum_f, cum_b)


def _block_diag(x, same_head):
    tiled = jnp.concatenate([x, x, x, x], axis=0)
    return jnp.where(same_head, tiled, 0.0).astype(BF16)


def _rwkv_scan_kernel(r_ref, v_ref, kk_ref, lin_ref, lex_ref, k_ref, b_ref, o_ref, s_ref, *, reverse):
    n_chunks = r_ref.shape[0] // CHUNK
    n_quads = RWKV_WIDTH // QUAD

    @pl.when(pl.program_id(1) == 0)
    def _():
        s_ref[...] = jnp.zeros_like(s_ref)

    row_h = lax.broadcasted_iota(jnp.int32, (QUAD, QUAD), 0) >> 6
    col_h = lax.broadcasted_iota(jnp.int32, (QUAD, QUAD), 1) >> 6
    same_head = row_h == col_h
    t_idx = lax.broadcasted_iota(jnp.int32, (CHUNK, QUAD), 0)
    s_idx = lax.broadcasted_iota(jnp.int32, (CHUNK, QUAD), 1) & (CHUNK - 1)
    if reverse:
        strict = s_idx > t_idx
        out_mask = strict
    else:
        strict = s_idx < t_idx
        out_mask = s_idx <= t_idx
    eye = (s_idx == t_idx).astype(F32)

    order = range(n_chunks - 1, -1, -1) if reverse else range(n_chunks)
    units = [(c, q) for c in order for q in range(n_quads)]
    n_sq = CHUNK.bit_length() - 1

    st = {}
    for c, q in units:
        rows = slice(c * CHUNK, (c + 1) * CHUNK)
        lanes = slice(q * QUAD, (q + 1) * QUAD)
        tot_row = c * CHUNK if reverse else (c + 1) * CHUNK - 1
        lin = lin_ref[rows, lanes]
        tot = lin_ref[tot_row:tot_row + 1, lanes]
        e_ex = jnp.exp(lex_ref[rows, lanes])
        e_neg = jnp.exp(-lin)
        e_end = jnp.exp(tot - lin)
        v = v_ref[rows, lanes].astype(F32)
        k = k_ref[rows, lanes].astype(F32)
        b = b_ref[rows, lanes].astype(F32)
        a_t = -kk_ref[rows, lanes].astype(F32) * e_ex
        r_t = r_ref[rows, lanes].astype(F32) * (e_ex if reverse else jnp.exp(lin))
        st[c, q] = dict(
            a_t=a_t, r_t=r_t, v=v, decay=jnp.exp(tot),
            ar=jnp.concatenate([a_t, r_t], axis=0).astype(BF16),
            bt_bd=_block_diag(b * e_neg, same_head), kt_bd=_block_diag(k * e_neg, same_head),
            v_bd=_block_diag(v, same_head),
            bk=jnp.concatenate([b * e_end, k * e_end], axis=0).astype(BF16))
    for u in units:
        s = st[u]
        sb = _dot_nt(s["ar"], s["bt_bd"])
        sk = _dot_nt(s["ar"], s["kt_bd"])
        a_ab = jnp.where(strict, sb[:CHUNK], 0.0)
        s.update(p=a_ab, t_inv=eye + a_ab, m_rb=jnp.where(out_mask, sb[CHUNK:], 0.0).astype(BF16),
                 akm=jnp.concatenate([jnp.where(strict, sk[:CHUNK], 0.0),
                                      jnp.where(out_mask, sk[CHUNK:], 0.0)], axis=0).astype(BF16))
    for u in units:
        s = st[u]
        kv = _dot(s["akm"], s["v_bd"])
        s["akv"] = kv[:CHUNK]
        s["y0"] = kv[CHUNK:]

    for j in range(n_sq - 1):
        for u in units:
            s = st[u]
            bd = _block_diag(s["p"], same_head)
            if j == 0:
                s["p"] = _dot(s["p"].astype(BF16), bd)
            else:
                tp = _dot(jnp.concatenate([s["t_inv"], s["p"]], axis=0).astype(BF16), bd)
                s["t_inv"] = s["t_inv"] + tp[:CHUNK]
                s["p"] = tp[CHUNK:]
    for u in units:
        s = st[u]
        t_inv = s["t_inv"] + _dot(s["t_inv"].astype(BF16), _block_diag(s["p"], same_head))
        s["t_b"] = t_inv.astype(BF16)
    for u in units:
        s = st[u]
        w_mat = _dot(s["t_b"], _block_diag(s["a_t"], same_head))
        s["u0"] = _dot(s["t_b"], _block_diag(s["akv"], same_head))
        s["wr"] = jnp.concatenate([w_mat, s["r_t"]], axis=0).astype(BF16)
    for u in units:
        s = st[u]
        s["g_mat"] = jnp.where(same_head, _dot_tn(s["wr"][:CHUNK], s["bk"][:CHUNK]), 0.0).astype(BF16)
        uv0 = jnp.concatenate([s["u0"], s["v"]], axis=0).astype(BF16)
        s["j_mat"] = jnp.where(same_head, _dot_tn(uv0, s["bk"]), 0.0)

    state = [s_ref[q] for q in range(n_quads)]

    def emit_y(c):
        for q in range(n_quads):
            s = st[c, q]
            y = s["y1"] + _dot(s["m_rb"], _block_diag(s["u_mat"], same_head))
            o_ref[c * CHUNK:(c + 1) * CHUNK, q * QUAD:(q + 1) * QUAD] = y.astype(o_ref.dtype)

    prev = None
    for c in order:
        for q in range(n_quads):
            s = st[c, q]
            s["s_in"] = state[q].astype(BF16)
            state[q] = state[q] * s["decay"] + _dot(s["s_in"], s["g_mat"]) + s["j_mat"]
        for q in range(n_quads):
            s = st[c, q]
            ws = _dot_nt(s["wr"], s["s_in"])
            s["u_mat"] = ws[:CHUNK] + s["u0"]
            s["y1"] = ws[CHUNK:] + s["y0"]
        if prev is not None:
            emit_y(prev)
        prev = c
    emit_y(prev)
    for q in range(n_quads):
        s_ref[q] = state[q]


def _rwkv_scan(r, v, kk, lin, lex, k, b, *, reverse):
    bsz, t, _ = r.shape
    tb = SCAN_BLOCK
    nblk = t // tb
    if reverse:
        idx = lambda bi, i: (bi, nblk - 1 - i, 0)
    else:
        idx = lambda bi, i: (bi, i, 0)
    spec = pl.BlockSpec((None, tb, RWKV_WIDTH), idx)
    return pl.pallas_call(
        functools.partial(_rwkv_scan_kernel, reverse=reverse),
        grid=(bsz, nblk),
        in_specs=[spec] * 7,
        out_specs=spec,
        out_shape=jax.ShapeDtypeStruct((bsz, t, RWKV_WIDTH), BF16),
        scratch_shapes=[pltpu.VMEM((RWKV_WIDTH // QUAD, QUAD, QUAD), F32)],
        compiler_params=_params(("parallel", "arbitrary")),
        name="rwkv_scan_bwd" if reverse else "rwkv_scan_fwd",
    )(r, v, kk, lin, lex, k, b)


def _na_kernel(q_ref, k0_ref, k1_ref, k2_ref, v0_ref, v1_ref, v2_ref, bias_ref, o_ref):
    k_refs = (k0_ref, k1_ref, k2_ref)
    v_refs = (v0_ref, v1_ref, v2_ref)
    lane_h = lax.broadcasted_iota(jnp.int32, (1, 128), 1) >> 6
    scale = HEAD_DIM ** -0.5
    blk = GRID_W * NA_QROWS
    group = 2
    for p0 in range(0, N_HEADS // 2, group):
        pairs = range(p0, p0 + group)
        heads = [(p, hh) for p in pairs for hh in range(2)]
        vp, s_all, e_all, l_all = {}, {}, {}, {}
        for p in pairs:
            lanes = slice(p * 128, (p + 1) * 128)
            qp = q_ref[:, lanes] * scale
            kp = [kr[:, lanes] for kr in k_refs]
            vp[p] = [vr[:, lanes] for vr in v_refs]
            for hh in range(2):
                qh = jnp.where(lane_h == hh, qp, 0).astype(BF16)
                s = jnp.concatenate([_dot_nt(qh, kj) for kj in kp], axis=1)
                s_all[p, hh] = s + bias_ref[2 * p + hh].astype(F32)
        for h in heads:
            m = jnp.max(s_all[h], axis=-1, keepdims=True)
            e = jnp.exp(s_all[h] - m)
            l_all[h] = jnp.sum(e, axis=-1, keepdims=True)
            e_all[h] = e.astype(BF16)
        for p in pairs:
            out_p = None
            for hh in range(2):
                eb = e_all[p, hh]
                pv = (_dot(eb[:, 0:blk], vp[p][0]) + _dot(eb[:, blk:2 * blk], vp[p][1])
                      + _dot(eb[:, 2 * blk:3 * blk], vp[p][2]))
                oh = pv / l_all[p, hh]
                out_p = oh if out_p is None else jnp.where(lane_h == hh, oh, out_p)
            o_ref[:, p * 128:(p + 1) * 128] = out_p.astype(o_ref.dtype)


def _na_attn(qkv3, bias):
    b, t, _ = qkv3.shape
    tq = NA_QROWS * GRID_W
    nblk = t // tq

    def kv_spec(col, j):
        return pl.BlockSpec((None, tq, NA_WIDTH),
                            lambda bi, i: (bi, jnp.clip(i - 1, 0, nblk - 3) + j, col))

    variant = lambda i: jnp.where(i == 0, 0, jnp.where(i == nblk - 1, 2, 1))
    return pl.pallas_call(
        _na_kernel,
        grid=(b, nblk),
        in_specs=[pl.BlockSpec((None, tq, NA_WIDTH), lambda bi, i: (bi, i, 0)),
                  kv_spec(1, 0), kv_spec(1, 1), kv_spec(1, 2),
                  kv_spec(2, 0), kv_spec(2, 1), kv_spec(2, 2),
                  pl.BlockSpec((None, N_HEADS, tq, NA_KROWS * GRID_W),
                               lambda bi, i: (variant(i), 0, 0, 0))],
        out_specs=pl.BlockSpec((None, tq, NA_WIDTH), lambda bi, i: (bi, i, 0)),
        out_shape=jax.ShapeDtypeStruct((b, t, NA_WIDTH), BF16),
        compiler_params=_params(("parallel", "arbitrary")),
        name="na_attn",
    )(qkv3, qkv3, qkv3, qkv3, qkv3, qkv3, qkv3, bias)


def _na_bias(rpb):
    j = np.arange(NA_QROWS)[:, None]
    kr = np.arange(NA_KROWS)[None, :]
    c = np.arange(GRID_W)[:, None]
    kc = np.arange(GRID_W)[None, :]
    cs = np.clip(c - NA_COLS // 2, 0, GRID_W - NA_COLS)
    col_ok = (kc >= cs) & (kc < cs + NA_COLS)
    dc = np.clip(kc - c + NA_COLS - 1, 0, 2 * NA_COLS - 2)
    col_sel = (dc[..., None] == np.arange(2 * NA_COLS - 1)) & col_ok[..., None]
    row_sel, masks = [], []
    for var in range(3):
        ws = (0 * j, j, 0 * j + 4)[var]
        qrel = (j, j + 4, j + 8)[var]
        row_ok = (kr >= ws) & (kr < ws + NA_ROWS)
        dr = np.clip(kr - qrel + NA_ROWS - 1, 0, 2 * NA_ROWS - 2)
        row_sel.append((dr[..., None] == np.arange(2 * NA_ROWS - 1)) & row_ok[..., None])
        masks.append(row_ok[:, None, :, None] & col_ok[None, :, None, :])
    row_sel = jnp.asarray(np.stack(row_sel), F32)
    mask = jnp.asarray(np.stack(masks))[:, None]
    vals = jnp.einsum("vjra,hab,ckb->vhjcrk", row_sel, rpb.astype(F32), jnp.asarray(col_sel, F32),
                      precision=lax.Precision.HIGHEST)
    bias = jnp.where(mask, vals, NEG_BIG).astype(BF16)
    return bias.reshape(3, N_HEADS, NA_QROWS * GRID_W, NA_KROWS * GRID_W)


def _merge_kernel(of_ref, ob_ref, bonus_ref, g_ref, un_ref, gate_ref, x_ref, lnw_ref, lnb_ref,
                  segm_ref, wa_ref, wn_ref, wo_ref, o_ref):
    o = of_ref[...].astype(F32) + ob_ref[...].astype(F32)
    segm = segm_ref[...]
    d = o - _split_dot(o, segm)
    var = _split_dot(d * d, segm)
    on = d * lax.rsqrt(var + GN_EPS)
    u_a = (on * lnw_ref[...] + lnb_ref[...] + bonus_ref[...]) * g_ref[...]
    gate = gate_ref[...]
    m = (gate[:, :D_MODEL] * _dot(u_a.astype(BF16), wa_ref[...])
         + gate[:, D_MODEL:] * _dot(un_ref[...], wn_ref[...]))
    o_ref[...] = x_ref[...] + _dot(m.astype(BF16), wo_ref[...])


def _merge(o_f, o_b, bonus, g, u_n, gates, x2, lnx_w, lnx_b, segm, w_a, w_n, w_o):
    n = x2.shape[0]
    tm = MATMUL_TILE
    row = lambda w: pl.BlockSpec((tm, w), lambda i: (i, 0))
    return pl.pallas_call(
        _merge_kernel,
        grid=(n // tm,),
        in_specs=[row(RWKV_WIDTH)] * 4 + [row(NA_WIDTH), row(GATE_IN), row(D_MODEL),
                  _const_spec((1, RWKV_WIDTH)), _const_spec((1, RWKV_WIDTH)),
                  _const_spec((RWKV_WIDTH, RWKV_WIDTH)), _const_spec((RWKV_WIDTH, D_MODEL)),
                  _const_spec((NA_WIDTH, D_MODEL)), _const_spec((D_MODEL, D_MODEL))],
        out_specs=row(D_MODEL),
        out_shape=jax.ShapeDtypeStruct((n, D_MODEL), F32),
        compiler_params=_params(("parallel",)),
        name="merge",
    )(o_f, o_b, bonus, g, u_n, gates, x2, lnx_w, lnx_b, segm, w_a, w_n, w_o)


def _ffn_tail_kernel(x_ref, p_ref, gf_ref, gp_ref, gl_ref, wg_ref, wu_ref, wd_ref, wple_ref, wpg_ref,
                     o_ref):
    x = x_ref[...]
    h = _rms(x, gf_ref[...]).astype(BF16)
    gt = _dot(h, wg_ref[...])
    act = gt * _sigmoid(gt) * _dot(h, wu_ref[...])
    x = x + _dot(act.astype(BF16), wd_ref[...])
    pg = _sigmoid(_dot(_rms(x, gp_ref[...]).astype(BF16), wpg_ref[...]))
    x = x + _dot(p_ref[...].astype(BF16), wple_ref[...]) * pg
    o_ref[...] = _rms(x, gl_ref[...])


def _ffn_tail(x2, p2, g_ffn, g_ple, g_final, w_gate, w_up, w_down, w_ple, w_pg):
    n = x2.shape[0]
    tm = MATMUL_TILE
    row = lambda w: pl.BlockSpec((tm, w), lambda i: (i, 0))
    vec = _const_spec((1, D_MODEL))
    return pl.pallas_call(
        _ffn_tail_kernel,
        grid=(n // tm,),
        in_specs=[row(D_MODEL), row(PLE_DIM), vec, vec, vec,
                  _const_spec((D_MODEL, D_FF)), _const_spec((D_MODEL, D_FF)),
                  _const_spec((D_FF, D_MODEL)), _const_spec((PLE_DIM, D_MODEL)),
                  _const_spec((D_MODEL, D_MODEL))],
        out_specs=row(D_MODEL),
        out_shape=jax.ShapeDtypeStruct((n, D_MODEL), F32),
        compiler_params=_params(("parallel",)),
        name="ffn_tail",
    )(x2, p2, g_ffn, g_ple, g_final, w_gate, w_up, w_down, w_ple, w_pg)


def _chunk_cumsum_mats(tm):
    t = np.arange(tm)
    same = (t[:, None] // CHUNK) == (t[None, :] // CHUNK)
    fwd = same & (t[None, :] <= t[:, None])
    bwd = same & (t[None, :] >= t[:, None])
    return jnp.asarray(fwd, BF16), jnp.asarray(bwd, BF16)


def _lora_pair(m_f, m_b):
    z = jnp.zeros_like(m_f)
    return jnp.concatenate([jnp.concatenate([m_f, z], axis=1),
                            jnp.concatenate([z, m_b], axis=1)], axis=0).astype(BF16)


def _prepare_weights(g_mix, w_in, mu_prev, mu_next, w0_f, w2_f, w0_b, w2_b, a0_f, a2_f, a0_b, a2_b,
                     g2, k_k, k_a, r_k, lnx_w, lnx_b, rpb, w_br_a, w_br_n, w_out, g_ffn, w_gate,
                     w_up, w_down, g_ple, w_ple, w_pg, g_final):
    w_in = w_in[0].astype(BF16)
    head = np.arange(RWKV_WIDTH) // HEAD_DIM
    seg = (head[:, None] == head[None, :]).astype(np.float32)
    cum_f, cum_b = _chunk_cumsum_mats(TOKEN_TILE)
    vecs = jnp.stack([w0_f[0], w0_b[0], a0_f[0], a0_b[0], k_k[0], k_a[0], r_k[0].reshape(-1),
                      jnp.zeros((RWKV_WIDTH,), F32)])
    return dict(
        g_mix=g_mix[0][None], w_r=w_in[:, :RWKV_IN], w_n=w_in[:, RWKV_IN:RWKV_IN + NA_IN],
        w_g=w_in[:, RWKV_IN + NA_IN:], mu_p=mu_prev[0][None], mu_n=mu_next[0][None], vecs=vecs,
        wlora=_lora_pair(w2_f[0], w2_b[0]), alora=_lora_pair(a2_f[0], a2_b[0]),
        g2=g2[0].astype(BF16), seg=jnp.asarray(seg, BF16), segm=jnp.asarray(seg / HEAD_DIM, BF16),
        cum_f=cum_f, cum_b=cum_b, bias=_na_bias(rpb[0]),
        lnx_w=lnx_w[0][None], lnx_b=lnx_b[0][None], w_a=w_br_a[0].astype(BF16),
        w_nb=w_br_n[0].astype(BF16), w_o=w_out[0].astype(BF16), g_ffn=g_ffn[0][None],
        g_ple=g_ple[0][None], g_final=g_final[None], w_gate=w_gate[0].astype(BF16),
        w_up=w_up[0].astype(BF16), w_down=w_down[0].astype(BF16), w_ple=w_ple[0].astype(BF16),
        w_pg=w_pg[0].astype(BF16))


def _trunk(x, p, w):
    b, t, _ = x.shape
    n = b * t
    x2 = x.reshape(n, D_MODEL)
    z_r, qkv, gates = _in_proj(x2, w["g_mix"], w["w_r"], w["w_n"], w["w_g"])
    (r, v, kk, lin_f, lex_f, k_f, b_f, lin_b, lex_b, k_b, b_b, bonus, g) = _rwkv_prep(
        z_r.reshape(b, t, RWKV_IN), w["mu_p"], w["mu_n"], w["vecs"], w["wlora"], w["alora"], w["g2"],
        w["seg"], w["cum_f"], w["cum_b"])
    o_f = _rwkv_scan(r, v, kk, lin_f, lex_f, k_f, b_f, reverse=False)
    o_b = _rwkv_scan(r, v, kk, lin_b, lex_b, k_b, b_b, reverse=True)
    u_n = _na_attn(qkv.reshape(b, t, NA_IN), w["bias"])
    flat = lambda a: a.reshape(n, a.shape[-1])
    x1 = _merge(flat(o_f), flat(o_b), flat(bonus), flat(g), flat(u_n), gates, x2, w["lnx_w"], w["lnx_b"],
                w["segm"], w["w_a"], w["w_nb"], w["w_o"])
    y = _ffn_tail(x1, p.reshape(n, PLE_DIM), w["g_ffn"], w["g_ple"], w["g_final"], w["w_gate"],
                  w["w_up"], w["w_down"], w["w_ple"], w["w_pg"])
    return y.reshape(b, t, D_MODEL)


def kernel(x_prompt, x_sample, p_prompt, p_sample, g_mix, w_in, mu_prev, mu_next, w0_f, w2_f, w0_b, w2_b, a0_f, a2_f, a0_b, a2_b, g2, k_k, k_a, r_k, lnx_w, lnx_b, rpb, w_br_a, w_br_n, w_out, g_ffn, w_gate, w_up, w_down, g_ple, w_ple, w_pg, g_final):
    w = _prepare_weights(g_mix, w_in, mu_prev, mu_next, w0_f, w2_f, w0_b, w2_b, a0_f, a2_f, a0_b, a2_b,
                         g2, k_k, k_a, r_k, lnx_w, lnx_b, rpb, w_br_a, w_br_n, w_out, g_ffn, w_gate,
                         w_up, w_down, g_ple, w_ple, w_pg, g_final)
    return (_trunk(x_prompt, p_prompt[0], w), _trunk(x_sample, p_sample[0], w))
```

```python
import functools

import numpy as np
import jax
import jax.numpy as jnp
from jax import lax
from jax.experimental import pallas as pl
from jax.experimental.pallas import tpu as pltpu

F32 = jnp.float32
BF16 = jnp.bfloat16

D_MODEL = 1024
HEAD_DIM = 64
RWKV_WIDTH = 512
NA_WIDTH = 512
N_HEADS = 8
RWKV_IN = 1920
NA_IN = 1536
GATE_IN = 2048
D_FF = 2816
PLE_DIM = 256
GRID_W = 64
NA_ROWS = 8
NA_COLS = 16
NORM_EPS = 1e-6
GN_EPS = 64e-5

CHUNK = 64
QUAD = 4 * HEAD_DIM
TOKEN_TILE = 256
MATMUL_TILE = 512
SCAN_BLOCK = 512
NA_QROWS = 4
NA_KROWS = 12
NEG_BIG = -1e30
LOG2_E = 1.4426950408889634
VMEM_LIMIT = 56 * 1024 * 1024


def _dot(a, b):
    return jnp.dot(a, b, preferred_element_type=F32)


def _dot_nt(a, b):
    return lax.dot_general(a, b, (((1,), (1,)), ((), ())), preferred_element_type=F32)


def _dot_tn(a, b):
    return lax.dot_general(a, b, (((0,), (0,)), ((), ())), preferred_element_type=F32)


def _split_dot(x, m):
    hi = x.astype(BF16)
    lo = (x - hi.astype(F32)).astype(BF16)
    return _dot(hi, m) + _dot(lo, m)


def _split3_dot(m, x):
    hi = x.astype(BF16)
    r1 = x - hi.astype(F32)
    mid = r1.astype(BF16)
    lo = (r1 - mid.astype(F32)).astype(BF16)
    return _dot(m, hi) + _dot(m, mid) + _dot(m, lo)


def _sigmoid(x):
    return 1.0 / (1.0 + jnp.exp(-x))


def _rms(x, g):
    ms = jnp.mean(x * x, axis=-1, keepdims=True)
    return x * lax.rsqrt(ms + NORM_EPS) * g


def _const_spec(shape):
    nd = len(shape)
    return pl.BlockSpec(shape, lambda *_: (0,) * nd, pipeline_mode=pl.Buffered(1))


def _params(sem):
    return pltpu.CompilerParams(dimension_semantics=sem, vmem_limit_bytes=VMEM_LIMIT)


def _in_proj_kernel(x_ref, g_ref, wr_ref, wn_ref, wg_ref, zr_ref, qkv_ref, gate_ref):
    h = _rms(x_ref[...], g_ref[...]).astype(BF16)
    zr_ref[...] = _dot(h, wr_ref[...])
    qkv_ref[...] = _dot(h, wn_ref[...]).astype(BF16)
    gate_ref[...] = _sigmoid(_dot(h, wg_ref[...])).astype(BF16)


def _in_proj(x2, g_mix, w_r, w_n, w_g):
    n = x2.shape[0]
    tm = MATMUL_TILE
    row = lambda w: pl.BlockSpec((tm, w), lambda i: (i, 0))
    return pl.pallas_call(
        _in_proj_kernel,
        grid=(n // tm,),
        in_specs=[row(D_MODEL), _const_spec((1, D_MODEL)), _const_spec((D_MODEL, RWKV_IN)),
                  _const_spec((D_MODEL, NA_IN)), _const_spec((D_MODEL, GATE_IN))],
        out_specs=[row(RWKV_IN), row(NA_IN), row(GATE_IN)],
        out_shape=[jax.ShapeDtypeStruct((n, RWKV_IN), F32),
                   jax.ShapeDtypeStruct((n, NA_IN), BF16),
                   jax.ShapeDtypeStruct((n, GATE_IN), BF16)],
        compiler_params=_params(("parallel",)),
        name="in_proj",
    )(x2, g_mix, w_r, w_n, w_g)


def _rwkv_prep_kernel(z_ref, zp_ref, zn_ref, mup_ref, mun_ref, vec_ref, wlora_ref, alora_ref,
                      g2_ref, seg_ref, cumf_ref, cumb_ref,
                      r_ref, v_ref, kk_ref, linf_ref, lexf_ref, kf_ref, bf_ref,
                      linb_ref, lexb_ref, kb_ref, bb_ref, bonus_ref, g_ref):
    i = pl.program_id(1)
    n_i = pl.num_programs(1)
    tm = z_ref.shape[0]
    z = z_ref[...]
    prev_row = jnp.where(i > 0, zp_ref[7:8, :], 0.0)
    next_row = jnp.where(i < n_i - 1, zn_ref[0:1, :], 0.0)
    rows = lax.broadcasted_iota(jnp.int32, (tm, 1), 0)
    zp = jnp.where(rows == 0, prev_row, pltpu.roll(z, 1, 0))
    zn = jnp.where(rows == tm - 1, next_row, pltpu.roll(z, tm - 1, 0))
    zs = z + mup_ref[...] * (zp - z) + mun_ref[...] * (zn - z)

    w = RWKV_WIDTH
    r, k, v = zs[:, 0:w], zs[:, w:2 * w], zs[:, 2 * w:3 * w]
    wd = zs[:, 3 * w:3 * w + 128]
    ad = zs[:, 3 * w + 128:3 * w + 256]
    gd = zs[:, 3 * w + 256:3 * w + 384]

    w0_f, w0_b = vec_ref[0:1, :], vec_ref[1:2, :]
    a0_f, a0_b = vec_ref[2:3, :], vec_ref[3:4, :]
    k_k, k_a, r_k = vec_ref[4:5, :], vec_ref[5:6, :], vec_ref[6:7, :]

    wl = _dot(jnp.tanh(wd).astype(BF16), wlora_ref[...])
    al = _dot(ad.astype(BF16), alora_ref[...])

    def log_decay(w0, lora):
        y = -(w0 + lora)
        softplus = jnp.maximum(y, 0.0) + jnp.log(1.0 + jnp.exp(-jnp.abs(y)))
        return -jnp.exp(-softplus - 0.5)

    lw_f = log_decay(w0_f, wl[:, 0:w])
    lw_b = log_decay(w0_b, wl[:, w:2 * w])
    a_f = _sigmoid(a0_f + al[:, 0:w])
    a_b = _sigmoid(a0_b + al[:, w:2 * w])
    g_ref[...] = _dot(_sigmoid(gd).astype(BF16), g2_ref[...]).astype(g_ref.dtype)

    k_f = k * (1.0 + (a_f - 1.0) * k_a)
    k_b = k * (1.0 + (a_b - 1.0) * k_a)
    kk = k * k_k
    seg = seg_ref[...]
    ss = _split_dot(kk * kk, seg)
    kk = kk / jnp.maximum(jnp.sqrt(ss), 1e-12)
    bonus_ref[...] = (_split_dot(r * (0.5 * (k_f + k_b)) * r_k, seg) * v).astype(bonus_ref.dtype)

    lin_f = _split3_dot(cumf_ref[...], lw_f)
    lin_b = _split3_dot(cumb_ref[...], lw_b)
    r_ref[...] = r.astype(r_ref.dtype)
    v_ref[...] = v.astype(v_ref.dtype)
    kk_ref[...] = kk.astype(kk_ref.dtype)
    linf_ref[...] = lin_f
    lexf_ref[...] = lin_f - lw_f
    kf_ref[...] = k_f.astype(kf_ref.dtype)
    bf_ref[...] = (kk * a_f).astype(bf_ref.dtype)
    linb_ref[...] = lin_b
    lexb_ref[...] = lin_b - lw_b
    kb_ref[...] = k_b.astype(kb_ref.dtype)
    bb_ref[...] = (kk * a_b).astype(bb_ref.dtype)


def _rwkv_prep(z3, mu_p, mu_n, vecs, wlora, alora, g2, seg, cum_f, cum_b):
    b, t, _ = z3.shape
    tm = TOKEN_TILE
    nb8 = tm // 8
    tile = lambda w: pl.BlockSpec((None, tm, w), lambda bi, i: (bi, i, 0))
    halo_p = pl.BlockSpec((None, 8, RWKV_IN), lambda bi, i: (bi, jnp.maximum(i * nb8 - 1, 0), 0))
    halo_n = pl.BlockSpec((None, 8, RWKV_IN),
                          lambda bi, i: (bi, jnp.minimum((i + 1) * nb8, t // 8 - 1), 0))
    out = lambda dt: jax.ShapeDtypeStruct((b, t, RWKV_WIDTH), dt)
    dtypes = [BF16] * 3 + [F32] * 2 + [BF16] * 2 + [F32] * 2 + [BF16] * 4
    return pl.pallas_call(
        _rwkv_prep_kernel,
        grid=(b, t // tm),
        in_specs=[tile(RWKV_IN), halo_p, halo_n, _const_spec((1, RWKV_IN)), _const_spec((1, RWKV_IN)),
                  _const_spec((8, RWKV_WIDTH)), _const_spec((128, 2 * RWKV_WIDTH)),
                  _const_spec((128, 2 * RWKV_WIDTH)), _const_spec((128, RWKV_WIDTH)),
                  _const_spec((RWKV_WIDTH, RWKV_WIDTH)), _const_spec((tm, tm)), _const_spec((tm, tm))],
        out_specs=[tile(RWKV_WIDTH)] * 13,
        out_shape=[out(dt) for dt in dtypes],
        compiler_params=_params(("parallel", "parallel")),
        name="rwkv_prep",
    )(z3, z3, z3, mu_p, mu_n, vecs, wlora, alora, g2, seg, cum_f, cum_b)


def _block_diag(x, same_head):
    tiled = jnp.concatenate([x, x, x, x], axis=0)
    return jnp.where(same_head, tiled, 0.0).astype(BF16)


def _rwkv_scan_kernel(r_ref, v_ref, kk_ref, lin_ref, lex_ref, k_ref, b_ref, o_ref, s_ref, *, reverse):
    n_chunks = r_ref.shape[0] // CHUNK
    n_quads = RWKV_WIDTH // QUAD

    @pl.when(pl.program_id(1) == 0)
    def _():
        s_ref[...] = jnp.zeros_like(s_ref)

    row_h = lax.broadcasted_iota(jnp.int32, (QUAD, QUAD), 0) >> 6
    col_h = lax.broadcasted_iota(jnp.int32, (QUAD, QUAD), 1) >> 6
    same_head = row_h == col_h
    t_idx = lax.broadcasted_iota(jnp.int32, (CHUNK, QUAD), 0)
    s_idx = lax.broadcasted_iota(jnp.int32, (CHUNK, QUAD), 1) & (CHUNK - 1)
    if reverse:
        strict = s_idx > t_idx
        out_mask = strict
    else:
        strict = s_idx < t_idx
        out_mask = s_idx <= t_idx
    eye = (s_idx == t_idx).astype(F32)

    order = range(n_chunks - 1, -1, -1) if reverse else range(n_chunks)
    units = [(c, q) for c in order for q in range(n_quads)]
    n_sq = CHUNK.bit_length() - 1

    st = {}
    for c, q in units:
        rows = slice(c * CHUNK, (c + 1) * CHUNK)
        lanes = slice(q * QUAD, (q + 1) * QUAD)
        tot_row = c * CHUNK if reverse else (c + 1) * CHUNK - 1
        lin = lin_ref[rows, lanes]
        tot = lin_ref[tot_row:tot_row + 1, lanes]
        e_ex = jnp.exp(lex_ref[rows, lanes])
        e_neg = jnp.exp(-lin)
        e_end = jnp.exp(tot - lin)
        v = v_ref[rows, lanes].astype(F32)
        k = k_ref[rows, lanes].astype(F32)
        b = b_ref[rows, lanes].astype(F32)
        a_t = -kk_ref[rows, lanes].astype(F32) * e_ex
        r_t = r_ref[rows, lanes].astype(F32) * (e_ex if reverse else jnp.exp(lin))
        st[c, q] = dict(
            a_t=a_t, r_t=r_t, v=v, decay=jnp.exp(tot),
            ar=jnp.concatenate([a_t, r_t], axis=0).astype(BF16),
            bt_bd=_block_diag(b * e_neg, same_head), kt_bd=_block_diag(k * e_neg, same_head),
            v_bd=_block_diag(v, same_head),
            bk=jnp.concatenate([b * e_end, k * e_end], axis=0).astype(BF16))
    for u in units:
        s = st[u]
        sb = _dot_nt(s["ar"], s["bt_bd"])
        sk = _dot_nt(s["ar"], s["kt_bd"])
        a_ab = jnp.where(strict, sb[:CHUNK], 0.0)
        s.update(p=a_ab, t_inv=eye + a_ab, m_rb=jnp.where(out_mask, sb[CHUNK:], 0.0).astype(BF16),
                 akm=jnp.concatenate([jnp.where(strict, sk[:CHUNK], 0.0),
                                      jnp.where(out_mask, sk[CHUNK:], 0.0)], axis=0).astype(BF16))
    for u in units:
        s = st[u]
        kv = _dot(s["akm"], s["v_bd"])
        s["akv"] = kv[:CHUNK]
        s["y0"] = kv[CHUNK:]

    for j in range(n_sq - 1):
        for u in units:
            s = st[u]
            bd = _block_diag(s["p"], same_head)
            if j == 0:
                s["p"] = _dot(s["p"].astype(BF16), bd)
            else:
                tp = _dot(jnp.concatenate([s["t_inv"], s["p"]], axis=0).astype(BF16), bd)
                s["t_inv"] = s["t_inv"] + tp[:CHUNK]
                s["p"] = tp[CHUNK:]
    for u in units:
        s = st[u]
        t_inv = s["t_inv"] + _dot(s["t_inv"].astype(BF16), _block_diag(s["p"], same_head))
        s["t_b"] = t_inv.astype(BF16)
    for u in units:
        s = st[u]
        w_mat = _dot(s["t_b"], _block_diag(s["a_t"], same_head))
        s["u0"] = _dot(s["t_b"], _block_diag(s["akv"], same_head))
        s["wr"] = jnp.concatenate([w_mat, s["r_t"]], axis=0).astype(BF16)
    for u in units:
        s = st[u]
        s["g_mat"] = jnp.where(same_head, _dot_tn(s["wr"][:CHUNK], s["bk"][:CHUNK]), 0.0).astype(BF16)
        uv0 = jnp.concatenate([s["u0"], s["v"]], axis=0).astype(BF16)
        s["j_mat"] = jnp.where(same_head, _dot_tn(uv0, s["bk"]), 0.0)

    state = [s_ref[q] for q in range(n_quads)]

    def emit_y(c):
        for q in range(n_quads):
            s = st[c, q]
            y = s["y1"] + _dot(s["m_rb"], _block_diag(s["u_mat"], same_head))
            o_ref[c * CHUNK:(c + 1) * CHUNK, q * QUAD:(q + 1) * QUAD] = y.astype(o_ref.dtype)

    prev = None
    for c in order:
        for q in range(n_quads):
            s = st[c, q]
            s["s_in"] = state[q].astype(BF16)
            state[q] = state[q] * s["decay"] + _dot(s["s_in"], s["g_mat"]) + s["j_mat"]
        for q in range(n_quads):
            s = st[c, q]
            ws = _dot_nt(s["wr"], s["s_in"])
            s["u_mat"] = ws[:CHUNK] + s["u0"]
            s["y1"] = ws[CHUNK:] + s["y0"]
        if prev is not None:
            emit_y(prev)
        prev = c
    emit_y(prev)
    for q in range(n_quads):
        s_ref[q] = state[q]


def _rwkv_scan(r, v, kk, lin, lex, k, b, *, reverse):
    bsz, t, _ = r.shape
    tb = SCAN_BLOCK
    nblk = t // tb
    if reverse:
        idx = lambda bi, i: (bi, nblk - 1 - i, 0)
    else:
        idx = lambda bi, i: (bi, i, 0)
    spec = pl.BlockSpec((None, tb, RWKV_WIDTH), idx)
    return pl.pallas_call(
        functools.partial(_rwkv_scan_kernel, reverse=reverse),
        grid=(bsz, nblk),
        in_specs=[spec] * 7,
        out_specs=spec,
        out_shape=jax.ShapeDtypeStruct((bsz, t, RWKV_WIDTH), BF16),
        scratch_shapes=[pltpu.VMEM((RWKV_WIDTH // QUAD, QUAD, QUAD), F32)],
        compiler_params=_params(("parallel", "arbitrary")),
        name="rwkv_scan_bwd" if reverse else "rwkv_scan_fwd",
    )(r, v, kk, lin, lex, k, b)


def _na_kernel(q_ref, k0_ref, k1_ref, k2_ref, v0_ref, v1_ref, v2_ref, bias_ref, o_ref):
    k_refs = (k0_ref, k1_ref, k2_ref)
    v_refs = (v0_ref, v1_ref, v2_ref)
    lane_h = lax.broadcasted_iota(jnp.int32, (1, 128), 1) >> 6
    scale = HEAD_DIM ** -0.5 * LOG2_E
    blk = GRID_W * NA_QROWS
    group = 2
    for p0 in range(0, N_HEADS // 2, group):
        pairs = range(p0, p0 + group)
        heads = [(p, hh) for p in pairs for hh in range(2)]
        vp, s_all, e_all, l_all = {}, {}, {}, {}
        for p in pairs:
            lanes = slice(p * 128, (p + 1) * 128)
            qp = q_ref[:, lanes].astype(F32) * scale
            kp = [kr[:, lanes] for kr in k_refs]
            vp[p] = [vr[:, lanes] for vr in v_refs]
            for hh in range(2):
                qh = jnp.where(lane_h == hh, qp, 0).astype(BF16)
                s = jnp.concatenate([_dot_nt(qh, kj) for kj in kp], axis=1)
                s_all[p, hh] = s + bias_ref[2 * p + hh]
        for h in heads:
            m = jnp.max(s_all[h], axis=-1, keepdims=True)
            e = jnp.exp2(s_all[h] - m)
            l_all[h] = jnp.sum(e, axis=-1, keepdims=True)
            e_all[h] = e.astype(BF16)
        for p in pairs:
            out_p = None
            for hh in range(2):
                eb = e_all[p, hh]
                pv = (_dot(eb[:, 0:blk], vp[p][0]) + _dot(eb[:, blk:2 * blk], vp[p][1])
                      + _dot(eb[:, 2 * blk:3 * blk], vp[p][2]))
                oh = pv / l_all[p, hh]
                out_p = oh if out_p is None else jnp.where(lane_h == hh, oh, out_p)
            o_ref[:, p * 128:(p + 1) * 128] = out_p.astype(o_ref.dtype)


def _na_attn(qkv3, bias):
    b, t, _ = qkv3.shape
    tq = NA_QROWS * GRID_W
    nblk = t // tq

    def kv_spec(col, j):
        return pl.BlockSpec((None, tq, NA_WIDTH),
                            lambda bi, i: (bi, jnp.clip(i - 1, 0, nblk - 3) + j, col))

    variant = lambda i: jnp.where(i == 0, 0, jnp.where(i == nblk - 1, 2, 1))
    return pl.pallas_call(
        _na_kernel,
        grid=(b, nblk),
        in_specs=[pl.BlockSpec((None, tq, NA_WIDTH), lambda bi, i: (bi, i, 0)),
                  kv_spec(1, 0), kv_spec(1, 1), kv_spec(1, 2),
                  kv_spec(2, 0), kv_spec(2, 1), kv_spec(2, 2),
                  pl.BlockSpec((None, N_HEADS, tq, NA_KROWS * GRID_W),
                               lambda bi, i: (variant(i), 0, 0, 0))],
        out_specs=pl.BlockSpec((None, tq, NA_WIDTH), lambda bi, i: (bi, i, 0)),
        out_shape=jax.ShapeDtypeStruct((b, t, NA_WIDTH), BF16),
        compiler_params=_params(("parallel", "arbitrary")),
        name="na_attn",
    )(qkv3, qkv3, qkv3, qkv3, qkv3, qkv3, qkv3, bias)


def _na_bias(rpb):
    j = np.arange(NA_QROWS)[:, None]
    kr = np.arange(NA_KROWS)[None, :]
    c = np.arange(GRID_W)[:, None]
    kc = np.arange(GRID_W)[None, :]
    cs = np.clip(c - NA_COLS // 2, 0, GRID_W - NA_COLS)
    col_ok = (kc >= cs) & (kc < cs + NA_COLS)
    dc = np.clip(kc - c + NA_COLS - 1, 0, 2 * NA_COLS - 2)
    col_sel = (dc[..., None] == np.arange(2 * NA_COLS - 1)) & col_ok[..., None]
    row_sel, masks = [], []
    for var in range(3):
        ws = (0 * j, j, 0 * j + 4)[var]
        qrel = (j, j + 4, j + 8)[var]
        row_ok = (kr >= ws) & (kr < ws + NA_ROWS)
        dr = np.clip(kr - qrel + NA_ROWS - 1, 0, 2 * NA_ROWS - 2)
        row_sel.append((dr[..., None] == np.arange(2 * NA_ROWS - 1)) & row_ok[..., None])
        masks.append(row_ok[:, None, :, None] & col_ok[None, :, None, :])
    row_sel = jnp.asarray(np.stack(row_sel), F32)
    mask = jnp.asarray(np.stack(masks))[:, None]
    vals = jnp.einsum("vjra,hab,ckb->vhjcrk", row_sel, rpb.astype(F32), jnp.asarray(col_sel, F32),
                      precision=lax.Precision.HIGHEST)
    bias = jnp.where(mask, vals * LOG2_E, NEG_BIG)
    return bias.reshape(3, N_HEADS, NA_QROWS * GRID_W, NA_KROWS * GRID_W)


def _merge_kernel(of_ref, ob_ref, bonus_ref, g_ref, un_ref, gate_ref, x_ref, lnw_ref, lnb_ref,
                  segm_ref, wa_ref, wn_ref, wo_ref, o_ref):
    o = of_ref[...].astype(F32) + ob_ref[...].astype(F32)
    segm = segm_ref[...]
    d = o - _split_dot(o, segm)
    var = _split_dot(d * d, segm)
    on = d * lax.rsqrt(var + GN_EPS)
    u_a = (on * lnw_ref[...] + lnb_ref[...] + bonus_ref[...]) * g_ref[...]
    gate = gate_ref[...]
    m = (gate[:, :D_MODEL] * _dot(u_a.astype(BF16), wa_ref[...])
         + gate[:, D_MODEL:] * _dot(un_ref[...], wn_ref[...]))
    o_ref[...] = x_ref[...] + _dot(m.astype(BF16), wo_ref[...])


def _merge(o_f, o_b, bonus, g, u_n, gates, x2, lnx_w, lnx_b, segm, w_a, w_n, w_o):
    n = x2.shape[0]
    tm = MATMUL_TILE
    row = lambda w: pl.BlockSpec((tm, w), lambda i: (i, 0))
    return pl.pallas_call(
        _merge_kernel,
        grid=(n // tm,),
        in_specs=[row(RWKV_WIDTH)] * 4 + [row(NA_WIDTH), row(GATE_IN), row(D_MODEL),
                  _const_spec((1, RWKV_WIDTH)), _const_spec((1, RWKV_WIDTH)),
                  _const_spec((RWKV_WIDTH, RWKV_WIDTH)), _const_spec((RWKV_WIDTH, D_MODEL)),
                  _const_spec((NA_WIDTH, D_MODEL)), _const_spec((D_MODEL, D_MODEL))],
        out_specs=row(D_MODEL),
        out_shape=jax.ShapeDtypeStruct((n, D_MODEL), F32),
        compiler_params=_params(("parallel",)),
        name="merge",
    )(o_f, o_b, bonus, g, u_n, gates, x2, lnx_w, lnx_b, segm, w_a, w_n, w_o)


def _ffn_tail_kernel(x_ref, p_ref, gf_ref, gp_ref, gl_ref, wg_ref, wu_ref, wd_ref, wple_ref, wpg_ref,
                     o_ref):
    x = x_ref[...]
    h = _rms(x, gf_ref[...]).astype(BF16)
    gt = _dot(h, wg_ref[...])
    act = gt * _sigmoid(gt) * _dot(h, wu_ref[...])
    x = x + _dot(act.astype(BF16), wd_ref[...])
    pg = _sigmoid(_dot(_rms(x, gp_ref[...]).astype(BF16), wpg_ref[...]))
    x = x + _dot(p_ref[...].astype(BF16), wple_ref[...]) * pg
    o_ref[...] = _rms(x, gl_ref[...])


def _ffn_tail(x2, p2, g_ffn, g_ple, g_final, w_gate, w_up, w_down, w_ple, w_pg):
    n = x2.shape[0]
    tm = MATMUL_TILE
    row = lambda w: pl.BlockSpec((tm, w), lambda i: (i, 0))
    vec = _const_spec((1, D_MODEL))
    return pl.pallas_call(
        _ffn_tail_kernel,
        grid=(n // tm,),
        in_specs=[row(D_MODEL), row(PLE_DIM), vec, vec, vec,
                  _const_spec((D_MODEL, D_FF)), _const_spec((D_MODEL, D_FF)),
                  _const_spec((D_FF, D_MODEL)), _const_spec((PLE_DIM, D_MODEL)),
                  _const_spec((D_MODEL, D_MODEL))],
        out_specs=row(D_MODEL),
        out_shape=jax.ShapeDtypeStruct((n, D_MODEL), F32),
        compiler_params=_params(("parallel",)),
        name="ffn_tail",
    )(x2, p2, g_ffn, g_ple, g_final, w_gate, w_up, w_down, w_ple, w_pg)


def _chunk_cumsum_mats(tm):
    t = np.arange(tm)
    same = (t[:, None] // CHUNK) == (t[None, :] // CHUNK)
    fwd = same & (t[None, :] <= t[:, None])
    bwd = same & (t[None, :] >= t[:, None])
    return jnp.asarray(fwd, BF16), jnp.asarray(bwd, BF16)


def _lora_pair(m_f, m_b):
    z = jnp.zeros_like(m_f)
    return jnp.concatenate([jnp.concatenate([m_f, z], axis=1),
                            jnp.concatenate([z, m_b], axis=1)], axis=0).astype(BF16)


def _prepare_weights(g_mix, w_in, mu_prev, mu_next, w0_f, w2_f, w0_b, w2_b, a0_f, a2_f, a0_b, a2_b,
                     g2, k_k, k_a, r_k, lnx_w, lnx_b, rpb, w_br_a, w_br_n, w_out, g_ffn, w_gate,
                     w_up, w_down, g_ple, w_ple, w_pg, g_final):
    w_in = w_in[0].astype(BF16)
    head = np.arange(RWKV_WIDTH) // HEAD_DIM
    seg = (head[:, None] == head[None, :]).astype(np.float32)
    cum_f, cum_b = _chunk_cumsum_mats(TOKEN_TILE)
    vecs = jnp.stack([w0_f[0], w0_b[0], a0_f[0], a0_b[0], k_k[0], k_a[0], r_k[0].reshape(-1),
                      jnp.zeros((RWKV_WIDTH,), F32)])
    return dict(
        g_mix=g_mix[0][None], w_r=w_in[:, :RWKV_IN], w_n=w_in[:, RWKV_IN:RWKV_IN + NA_IN],
        w_g=w_in[:, RWKV_IN + NA_IN:], mu_p=mu_prev[0][None], mu_n=mu_next[0][None], vecs=vecs,
        wlora=_lora_pair(w2_f[0], w2_b[0]), alora=_lora_pair(a2_f[0], a2_b[0]),
        g2=g2[0].astype(BF16), seg=jnp.asarray(seg, BF16), segm=jnp.asarray(seg / HEAD_DIM, BF16),
        cum_f=cum_f, cum_b=cum_b, bias=_na_bias(rpb[0]),
        lnx_w=lnx_w[0][None], lnx_b=lnx_b[0][None], w_a=w_br_a[0].astype(BF16),
        w_nb=w_br_n[0].astype(BF16), w_o=w_out[0].astype(BF16), g_ffn=g_ffn[0][None],
        g_ple=g_ple[0][None], g_final=g_final[None], w_gate=w_gate[0].astype(BF16),
        w_up=w_up[0].astype(BF16), w_down=w_down[0].astype(BF16), w_ple=w_ple[0].astype(BF16),
        w_pg=w_pg[0].astype(BF16))


def _trunk(x, p, w):
    b, t, _ = x.shape
    n = b * t
    x2 = x.reshape(n, D_MODEL)
    z_r, qkv, gates = _in_proj(x2, w["g_mix"], w["w_r"], w["w_n"], w["w_g"])
    (r, v, kk, lin_f, lex_f, k_f, b_f, lin_b, lex_b, k_b, b_b, bonus, g) = _rwkv_prep(
        z_r.reshape(b, t, RWKV_IN), w["mu_p"], w["mu_n"], w["vecs"], w["wlora"], w["alora"], w["g2"],
        w["seg"], w["cum_f"], w["cum_b"])
    o_f = _rwkv_scan(r, v, kk, lin_f, lex_f, k_f, b_f, reverse=False)
    o_b = _rwkv_scan(r, v, kk, lin_b, lex_b, k_b, b_b, reverse=True)
    u_n = _na_attn(qkv.reshape(b, t, NA_IN), w["bias"])
    flat = lambda a: a.reshape(n, a.shape[-1])
    x1 = _merge(flat(o_f), flat(o_b), flat(bonus), flat(g), flat(u_n), gates, x2, w["lnx_w"], w["lnx_b"],
                w["segm"], w["w_a"], w["w_nb"], w["w_o"])
    y = _ffn_tail(x1, p.reshape(n, PLE_DIM), w["g_ffn"], w["g_ple"], w["g_final"], w["w_gate"],
                  w["w_up"], w["w_down"], w["w_ple"], w["w_pg"])
    return y.reshape(b, t, D_MODEL)


def kernel(x_prompt, x_sample, p_prompt, p_sample, g_mix, w_in, mu_prev, mu_next, w0_f, w2_f, w0_b, w2_b, a0_f, a2_f, a0_b, a2_b, g2, k_k, k_a, r_k, lnx_w, lnx_b, rpb, w_br_a, w_br_n, w_out, g_ffn, w_gate, w_up, w_down, g_ple, w_ple, w_pg, g_final):
    w = _prepare_weights(g_mix, w_in, mu_prev, mu_next, w0_f, w2_f, w0_b, w2_b, a0_f, a2_f, a0_b, a2_b,
                         g2, k_k, k_a, r_k, lnx_w, lnx_b, rpb, w_br_a, w_br_n, w_out, g_ffn, w_gate,
                         w_up, w_down, g_ple, w_ple, w_pg, g_final)
    return (_trunk(x_prompt, p_prompt[0], w), _trunk(x_sample, p_sample[0], w))
```

```python
import functools

import numpy as np
import jax
import jax.numpy as jnp
from jax import lax
from jax.experimental import pallas as pl
from jax.experimental.pallas import tpu as pltpu

F32 = jnp.float32
BF16 = jnp.bfloat16

D_MODEL = 1024
HEAD_DIM = 64
RWKV_WIDTH = 512
NA_WIDTH = 512
N_HEADS = 8
RWKV_IN = 1920
NA_IN = 1536
GATE_IN = 2048
D_FF = 2816
PLE_DIM = 256
GRID_W = 64
NA_ROWS = 8
NA_COLS = 16
NORM_EPS = 1e-6
GN_EPS = 64e-5

CHUNK = 64
QUAD = 4 * HEAD_DIM
TOKEN_TILE = 256
MATMUL_TILE = 512
SCAN_BLOCK = 1024
NA_QROWS = 4
NA_KROWS = 12
NEG_BIG = -1e30
LOG2_E = 1.4426950408889634
VMEM_LIMIT = 56 * 1024 * 1024


def _dot(a, b):
    return jnp.dot(a, b, preferred_element_type=F32)


def _dot_nt(a, b):
    return lax.dot_general(a, b, (((1,), (1,)), ((), ())), preferred_element_type=F32)


def _dot_tn(a, b):
    return lax.dot_general(a, b, (((0,), (0,)), ((), ())), preferred_element_type=F32)


def _split_dot(x, m):
    hi = x.astype(BF16)
    lo = (x - hi.astype(F32)).astype(BF16)
    return _dot(hi, m) + _dot(lo, m)


def _split3_dot(m, x):
    hi = x.astype(BF16)
    r1 = x - hi.astype(F32)
    mid = r1.astype(BF16)
    lo = (r1 - mid.astype(F32)).astype(BF16)
    return _dot(m, hi) + _dot(m, mid) + _dot(m, lo)


def _sigmoid(x):
    return 1.0 / (1.0 + jnp.exp(-x))


def _rms(x, g):
    ms = jnp.mean(x * x, axis=-1, keepdims=True)
    return x * lax.rsqrt(ms + NORM_EPS) * g


def _const_spec(shape):
    nd = len(shape)
    return pl.BlockSpec(shape, lambda *_: (0,) * nd, pipeline_mode=pl.Buffered(1))


def _params(sem):
    return pltpu.CompilerParams(dimension_semantics=sem, vmem_limit_bytes=VMEM_LIMIT)


def _in_proj_kernel(x_ref, g_ref, wr_ref, wn_ref, wg_ref, zr_ref, qkv_ref, gate_ref):
    h = _rms(x_ref[...], g_ref[...]).astype(BF16)
    zr_ref[...] = _dot(h, wr_ref[...])
    qkv_ref[...] = _dot(h, wn_ref[...]).astype(BF16)
    gate_ref[...] = _sigmoid(_dot(h, wg_ref[...])).astype(BF16)


def _in_proj(x2, g_mix, w_r, w_n, w_g):
    n = x2.shape[0]
    tm = MATMUL_TILE
    row = lambda w: pl.BlockSpec((tm, w), lambda i: (i, 0))
    return pl.pallas_call(
        _in_proj_kernel,
        grid=(n // tm,),
        in_specs=[row(D_MODEL), _const_spec((1, D_MODEL)), _const_spec((D_MODEL, RWKV_IN)),
                  _const_spec((D_MODEL, NA_IN)), _const_spec((D_MODEL, GATE_IN))],
        out_specs=[row(RWKV_IN), row(NA_IN), row(GATE_IN)],
        out_shape=[jax.ShapeDtypeStruct((n, RWKV_IN), F32),
                   jax.ShapeDtypeStruct((n, NA_IN), BF16),
                   jax.ShapeDtypeStruct((n, GATE_IN), BF16)],
        compiler_params=_params(("parallel",)),
        name="in_proj",
    )(x2, g_mix, w_r, w_n, w_g)


def _rwkv_prep_kernel(z_ref, zp_ref, zn_ref, mup_ref, mun_ref, vec_ref, wlora_ref, alora_ref,
                      g2_ref, seg_ref, cumf_ref, cumb_ref,
                      r_ref, v_ref, kk_ref, linf_ref, lexf_ref, kf_ref, bf_ref,
                      linb_ref, lexb_ref, kb_ref, bb_ref, bonus_ref, g_ref):
    i = pl.program_id(1)
    n_i = pl.num_programs(1)
    tm = z_ref.shape[0]
    z = z_ref[...]
    prev_row = jnp.where(i > 0, zp_ref[7:8, :], 0.0)
    next_row = jnp.where(i < n_i - 1, zn_ref[0:1, :], 0.0)
    rows = lax.broadcasted_iota(jnp.int32, (tm, 1), 0)
    zp = jnp.where(rows == 0, prev_row, pltpu.roll(z, 1, 0))
    zn = jnp.where(rows == tm - 1, next_row, pltpu.roll(z, tm - 1, 0))
    mu_p, mu_n = mup_ref[...], mun_ref[...]
    zs = z * (1.0 - mu_p - mu_n) + mu_p * zp + mu_n * zn

    w = RWKV_WIDTH
    r, k, v = zs[:, 0:w], zs[:, w:2 * w], zs[:, 2 * w:3 * w]
    wd = zs[:, 3 * w:3 * w + 128]
    ad = zs[:, 3 * w + 128:3 * w + 256]
    gd = zs[:, 3 * w + 256:3 * w + 384]

    w0_f, w0_b = vec_ref[0:1, :], vec_ref[1:2, :]
    a0_f, a0_b = vec_ref[2:3, :], vec_ref[3:4, :]
    k_k, k_a, r_k = vec_ref[4:5, :], vec_ref[5:6, :], vec_ref[6:7, :]

    wl = _dot(jnp.tanh(wd).astype(BF16), wlora_ref[...])
    al = _dot(ad.astype(BF16), alora_ref[...])

    def log_decay(w0, lora):
        y = -(w0 + lora)
        softplus = jnp.maximum(y, 0.0) + jnp.log(1.0 + jnp.exp(-jnp.abs(y)))
        return -jnp.exp(-softplus - 0.5)

    lw_f = log_decay(w0_f, wl[:, 0:w])
    lw_b = log_decay(w0_b, wl[:, w:2 * w])
    a_f = _sigmoid(a0_f + al[:, 0:w])
    a_b = _sigmoid(a0_b + al[:, w:2 * w])
    g_ref[...] = _dot(_sigmoid(gd).astype(BF16), g2_ref[...]).astype(g_ref.dtype)

    k_f = k * (1.0 + (a_f - 1.0) * k_a)
    k_b = k * (1.0 + (a_b - 1.0) * k_a)
    kk = k * k_k
    seg = seg_ref[...]
    ss = _split_dot(kk * kk, seg)
    kk = kk / jnp.maximum(jnp.sqrt(ss), 1e-12)
    bonus_ref[...] = (_split_dot(r * (0.5 * (k_f + k_b)) * r_k, seg) * v).astype(bonus_ref.dtype)

    lin_f = _split3_dot(cumf_ref[...], lw_f)
    lin_b = _split3_dot(cumb_ref[...], lw_b)
    r_ref[...] = r.astype(r_ref.dtype)
    v_ref[...] = v.astype(v_ref.dtype)
    kk_ref[...] = kk.astype(kk_ref.dtype)
    linf_ref[...] = lin_f
    lexf_ref[...] = lin_f - lw_f
    kf_ref[...] = k_f.astype(kf_ref.dtype)
    bf_ref[...] = (kk * a_f).astype(bf_ref.dtype)
    linb_ref[...] = lin_b
    lexb_ref[...] = lin_b - lw_b
    kb_ref[...] = k_b.astype(kb_ref.dtype)
    bb_ref[...] = (kk * a_b).astype(bb_ref.dtype)


def _rwkv_prep(z3, mu_p, mu_n, vecs, wlora, alora, g2, seg, cum_f, cum_b):
    b, t, _ = z3.shape
    tm = TOKEN_TILE
    nb8 = tm // 8
    tile = lambda w: pl.BlockSpec((None, tm, w), lambda bi, i: (bi, i, 0))
    halo_p = pl.BlockSpec((None, 8, RWKV_IN), lambda bi, i: (bi, jnp.maximum(i * nb8 - 1, 0), 0))
    halo_n = pl.BlockSpec((None, 8, RWKV_IN),
                          lambda bi, i: (bi, jnp.minimum((i + 1) * nb8, t // 8 - 1), 0))
    out = lambda dt: jax.ShapeDtypeStruct((b, t, RWKV_WIDTH), dt)
    dtypes = [BF16] * 3 + [F32] * 2 + [BF16] * 2 + [F32] * 2 + [BF16] * 4
    return pl.pallas_call(
        _rwkv_prep_kernel,
        grid=(b, t // tm),
        in_specs=[tile(RWKV_IN), halo_p, halo_n, _const_spec((1, RWKV_IN)), _const_spec((1, RWKV_IN)),
                  _const_spec((8, RWKV_WIDTH)), _const_spec((128, 2 * RWKV_WIDTH)),
                  _const_spec((128, 2 * RWKV_WIDTH)), _const_spec((128, RWKV_WIDTH)),
                  _const_spec((RWKV_WIDTH, RWKV_WIDTH)), _const_spec((tm, tm)), _const_spec((tm, tm))],
        out_specs=[tile(RWKV_WIDTH)] * 13,
        out_shape=[out(dt) for dt in dtypes],
        compiler_params=_params(("parallel", "parallel")),
        name="rwkv_prep",
    )(z3, z3, z3, mu_p, mu_n, vecs, wlora, alora, g2, seg, cum_f, cum_b)


def _block_diag(x, same_head):
    tiled = jnp.concatenate([x, x, x, x], axis=0)
    return jnp.where(same_head, tiled, 0.0).astype(BF16)


def _rwkv_scan_kernel(r_ref, v_ref, kk_ref, lin_ref, lex_ref, k_ref, b_ref, o_ref, s_ref, *, reverse):
    n_chunks = r_ref.shape[0] // CHUNK
    n_quads = RWKV_WIDTH // QUAD

    @pl.when(pl.program_id(1) == 0)
    def _():
        s_ref[...] = jnp.zeros_like(s_ref)

    row_h = lax.broadcasted_iota(jnp.int32, (QUAD, QUAD), 0) >> 6
    col_h = lax.broadcasted_iota(jnp.int32, (QUAD, QUAD), 1) >> 6
    same_head = row_h == col_h
    t_idx = lax.broadcasted_iota(jnp.int32, (CHUNK, QUAD), 0)
    s_idx = lax.broadcasted_iota(jnp.int32, (CHUNK, QUAD), 1) & (CHUNK - 1)
    if reverse:
        strict = s_idx > t_idx
        out_mask = strict
    else:
        strict = s_idx < t_idx
        out_mask = s_idx <= t_idx
    eye = (s_idx == t_idx).astype(F32)

    order = range(n_chunks - 1, -1, -1) if reverse else range(n_chunks)
    units = [(c, q) for c in order for q in range(n_quads)]
    n_sq = CHUNK.bit_length() - 1

    st = {}
    for c, q in units:
        rows = slice(c * CHUNK, (c + 1) * CHUNK)
        lanes = slice(q * QUAD, (q + 1) * QUAD)
        tot_row = c * CHUNK if reverse else (c + 1) * CHUNK - 1
        lin = lin_ref[rows, lanes]
        tot = lin_ref[tot_row:tot_row + 1, lanes]
        e_ex = jnp.exp(lex_ref[rows, lanes])
        e_neg = jnp.exp(-lin)
        e_end = jnp.exp(tot - lin)
        v = v_ref[rows, lanes].astype(F32)
        k = k_ref[rows, lanes].astype(F32)
        b = b_ref[rows, lanes].astype(F32)
        a_t = -kk_ref[rows, lanes].astype(F32) * e_ex
        r_t = r_ref[rows, lanes].astype(F32) * (e_ex if reverse else jnp.exp(lin))
        st[c, q] = dict(
            a_t=a_t, r_t=r_t, v=v, decay=jnp.exp(tot),
            ar=jnp.concatenate([a_t, r_t], axis=0).astype(BF16),
            bt_bd=_block_diag(b * e_neg, same_head), kt_bd=_block_diag(k * e_neg, same_head),
            v_bd=_block_diag(v, same_head),
            bk=jnp.concatenate([b * e_end, k * e_end], axis=0).astype(BF16))
    for u in units:
        s = st[u]
        sb = _dot_nt(s["ar"], s["bt_bd"])
        sk = _dot_nt(s["ar"], s["kt_bd"])
        a_ab = jnp.where(strict, sb[:CHUNK], 0.0)
        s.update(p=a_ab, t_inv=eye + a_ab, m_rb=jnp.where(out_mask, sb[CHUNK:], 0.0).astype(BF16),
                 akm=jnp.concatenate([jnp.where(strict, sk[:CHUNK], 0.0),
                                      jnp.where(out_mask, sk[CHUNK:], 0.0)], axis=0).astype(BF16))
    for u in units:
        s = st[u]
        kv = _dot(s["akm"], s["v_bd"])
        s["akv"] = kv[:CHUNK]
        s["y0"] = kv[CHUNK:]

    for j in range(n_sq - 1):
        for u in units:
            s = st[u]
            bd = _block_diag(s["p"], same_head)
            if j == 0:
                s["p"] = _dot(s["p"].astype(BF16), bd)
            else:
                tp = _dot(jnp.concatenate([s["t_inv"], s["p"]], axis=0).astype(BF16), bd)
                s["t_inv"] = s["t_inv"] + tp[:CHUNK]
                s["p"] = tp[CHUNK:]
    for u in units:
        s = st[u]
        t_inv = s["t_inv"] + _dot(s["t_inv"].astype(BF16), _block_diag(s["p"], same_head))
        s["t_b"] = t_inv.astype(BF16)
    for u in units:
        s = st[u]
        w_mat = _dot(s["t_b"], _block_diag(s["a_t"], same_head))
        s["u0"] = _dot(s["t_b"], _block_diag(s["akv"], same_head))
        s["wr"] = jnp.concatenate([w_mat, s["r_t"]], axis=0).astype(BF16)
    for u in units:
        s = st[u]
        s["g_mat"] = jnp.where(same_head, _dot_tn(s["wr"][:CHUNK], s["bk"][:CHUNK]), 0.0).astype(BF16)
        uv0 = jnp.concatenate([s["u0"], s["v"]], axis=0).astype(BF16)
        s["j_mat"] = jnp.where(same_head, _dot_tn(uv0, s["bk"]), 0.0)

    state = [s_ref[q] for q in range(n_quads)]

    def emit_y(c):
        for q in range(n_quads):
            s = st[c, q]
            y = s["y1"] + _dot(s["m_rb"], _block_diag(s["u_mat"], same_head))
            o_ref[c * CHUNK:(c + 1) * CHUNK, q * QUAD:(q + 1) * QUAD] = y.astype(o_ref.dtype)

    prev = None
    for c in order:
        for q in range(n_quads):
            s = st[c, q]
            s["s_in"] = state[q].astype(BF16)
            state[q] = state[q] * s["decay"] + _dot(s["s_in"], s["g_mat"]) + s["j_mat"]
        for q in range(n_quads):
            s = st[c, q]
            ws = _dot_nt(s["wr"], s["s_in"])
            s["u_mat"] = ws[:CHUNK] + s["u0"]
            s["y1"] = ws[CHUNK:] + s["y0"]
        if prev is not None:
            emit_y(prev)
        prev = c
    emit_y(prev)
    for q in range(n_quads):
        s_ref[q] = state[q]


def _rwkv_scan(r, v, kk, lin, lex, k, b, *, reverse):
    bsz, t, _ = r.shape
    tb = SCAN_BLOCK
    nblk = t // tb
    if reverse:
        idx = lambda bi, i: (bi, nblk - 1 - i, 0)
    else:
        idx = lambda bi, i: (bi, i, 0)
    spec = pl.BlockSpec((None, tb, RWKV_WIDTH), idx)
    return pl.pallas_call(
        functools.partial(_rwkv_scan_kernel, reverse=reverse),
        grid=(bsz, nblk),
        in_specs=[spec] * 7,
        out_specs=spec,
        out_shape=jax.ShapeDtypeStruct((bsz, t, RWKV_WIDTH), BF16),
        scratch_shapes=[pltpu.VMEM((RWKV_WIDTH // QUAD, QUAD, QUAD), F32)],
        compiler_params=_params(("parallel", "arbitrary")),
        name="rwkv_scan_bwd" if reverse else "rwkv_scan_fwd",
    )(r, v, kk, lin, lex, k, b)


def _na_kernel(q_ref, k0_ref, k1_ref, k2_ref, v0_ref, v1_ref, v2_ref, bias_ref, o_ref):
    k_refs = (k0_ref, k1_ref, k2_ref)
    v_refs = (v0_ref, v1_ref, v2_ref)
    lane_h = lax.broadcasted_iota(jnp.int32, (1, 128), 1) >> 6
    scale = HEAD_DIM ** -0.5 * LOG2_E
    blk = GRID_W * NA_QROWS
    group = 2
    for p0 in range(0, N_HEADS // 2, group):
        pairs = range(p0, p0 + group)
        heads = [(p, hh) for p in pairs for hh in range(2)]
        vp, s_all, e_all, l_all = {}, {}, {}, {}
        for p in pairs:
            lanes = slice(p * 128, (p + 1) * 128)
            qp = q_ref[:, lanes].astype(F32) * scale
            kp = [kr[:, lanes] for kr in k_refs]
            vp[p] = [vr[:, lanes] for vr in v_refs]
            for hh in range(2):
                qh = jnp.where(lane_h == hh, qp, 0).astype(BF16)
                s = jnp.concatenate([_dot_nt(qh, kj) for kj in kp], axis=1)
                s_all[p, hh] = s + bias_ref[2 * p + hh]
        for h in heads:
            m = jnp.max(s_all[h], axis=-1, keepdims=True)
            e = jnp.exp2(s_all[h] - m)
            l_all[h] = jnp.sum(e, axis=-1, keepdims=True)
            e_all[h] = e.astype(BF16)
        for p in pairs:
            out_p = None
            for hh in range(2):
                eb = e_all[p, hh]
                pv = (_dot(eb[:, 0:blk], vp[p][0]) + _dot(eb[:, blk:2 * blk], vp[p][1])
                      + _dot(eb[:, 2 * blk:3 * blk], vp[p][2]))
                oh = pv / l_all[p, hh]
                out_p = oh if out_p is None else jnp.where(lane_h == hh, oh, out_p)
            o_ref[:, p * 128:(p + 1) * 128] = out_p.astype(o_ref.dtype)


def _na_attn(qkv3, bias):
    b, t, _ = qkv3.shape
    tq = NA_QROWS * GRID_W
    nblk = t // tq

    def kv_spec(col, j):
        return pl.BlockSpec((None, tq, NA_WIDTH),
                            lambda bi, i: (bi, jnp.clip(i - 1, 0, nblk - 3) + j, col))

    variant = lambda i: jnp.where(i == 0, 0, jnp.where(i == nblk - 1, 2, 1))
    return pl.pallas_call(
        _na_kernel,
        grid=(b, nblk),
        in_specs=[pl.BlockSpec((None, tq, NA_WIDTH), lambda bi, i: (bi, i, 0)),
                  kv_spec(1, 0), kv_spec(1, 1), kv_spec(1, 2),
                  kv_spec(2, 0), kv_spec(2, 1), kv_spec(2, 2),
                  pl.BlockSpec((None, N_HEADS, tq, NA_KROWS * GRID_W),
                               lambda bi, i: (variant(i), 0, 0, 0))],
        out_specs=pl.BlockSpec((None, tq, NA_WIDTH), lambda bi, i: (bi, i, 0)),
        out_shape=jax.ShapeDtypeStruct((b, t, NA_WIDTH), BF16),
        compiler_params=_params(("parallel", "arbitrary")),
        name="na_attn",
    )(qkv3, qkv3, qkv3, qkv3, qkv3, qkv3, qkv3, bias)


def _na_bias(rpb):
    j = np.arange(NA_QROWS)[:, None]
    kr = np.arange(NA_KROWS)[None, :]
    c = np.arange(GRID_W)[:, None]
    kc = np.arange(GRID_W)[None, :]
    cs = np.clip(c - NA_COLS // 2, 0, GRID_W - NA_COLS)
    col_ok = (kc >= cs) & (kc < cs + NA_COLS)
    dc = np.clip(kc - c + NA_COLS - 1, 0, 2 * NA_COLS - 2)
    col_sel = (dc[..., None] == np.arange(2 * NA_COLS - 1)) & col_ok[..., None]
    row_sel, masks = [], []
    for var in range(3):
        ws = (0 * j, j, 0 * j + 4)[var]
        qrel = (j, j + 4, j + 8)[var]
        row_ok = (kr >= ws) & (kr < ws + NA_ROWS)
        dr = np.clip(kr - qrel + NA_ROWS - 1, 0, 2 * NA_ROWS - 2)
        row_sel.append((dr[..., None] == np.arange(2 * NA_ROWS - 1)) & row_ok[..., None])
        masks.append(row_ok[:, None, :, None] & col_ok[None, :, None, :])
    row_sel = jnp.asarray(np.stack(row_sel), F32)
    mask = jnp.asarray(np.stack(masks))[:, None]
    vals = jnp.einsum("vjra,hab,ckb->vhjcrk", row_sel, rpb.astype(F32), jnp.asarray(col_sel, F32),
                      precision=lax.Precision.HIGHEST)
    bias = jnp.where(mask, vals * LOG2_E, NEG_BIG)
    return bias.reshape(3, N_HEADS, NA_QROWS * GRID_W, NA_KROWS * GRID_W)


def _merge_kernel(of_ref, ob_ref, bonus_ref, g_ref, un_ref, gate_ref, x_ref, lnw_ref, lnb_ref,
                  segm_ref, wa_ref, wn_ref, wo_ref, o_ref):
    o = of_ref[...].astype(F32) + ob_ref[...].astype(F32)
    segm = segm_ref[...]
    d = o - _split_dot(o, segm)
    var = _split_dot(d * d, segm)
    on = d * lax.rsqrt(var + GN_EPS)
    u_a = (on * lnw_ref[...] + lnb_ref[...] + bonus_ref[...]) * g_ref[...]
    gate = gate_ref[...]
    m = (gate[:, :D_MODEL] * _dot(u_a.astype(BF16), wa_ref[...])
         + gate[:, D_MODEL:] * _dot(un_ref[...], wn_ref[...]))
    o_ref[...] = x_ref[...] + _dot(m.astype(BF16), wo_ref[...])


def _merge(o_f, o_b, bonus, g, u_n, gates, x2, lnx_w, lnx_b, segm, w_a, w_n, w_o):
    n = x2.shape[0]
    tm = MATMUL_TILE
    row = lambda w: pl.BlockSpec((tm, w), lambda i: (i, 0))
    return pl.pallas_call(
        _merge_kernel,
        grid=(n // tm,),
        in_specs=[row(RWKV_WIDTH)] * 4 + [row(NA_WIDTH), row(GATE_IN), row(D_MODEL),
                  _const_spec((1, RWKV_WIDTH)), _const_spec((1, RWKV_WIDTH)),
                  _const_spec((RWKV_WIDTH, RWKV_WIDTH)), _const_spec((RWKV_WIDTH, D_MODEL)),
                  _const_spec((NA_WIDTH, D_MODEL)), _const_spec((D_MODEL, D_MODEL))],
        out_specs=row(D_MODEL),
        out_shape=jax.ShapeDtypeStruct((n, D_MODEL), F32),
        compiler_params=_params(("parallel",)),
        name="merge",
    )(o_f, o_b, bonus, g, u_n, gates, x2, lnx_w, lnx_b, segm, w_a, w_n, w_o)


def _ffn_tail_kernel(x_ref, p_ref, gf_ref, gp_ref, gl_ref, wg_ref, wu_ref, wd_ref, wple_ref, wpg_ref,
                     o_ref):
    x = x_ref[...]
    h = _rms(x, gf_ref[...]).astype(BF16)
    gt = _dot(h, wg_ref[...])
    act = gt * _sigmoid(gt) * _dot(h, wu_ref[...])
    x = x + _dot(act.astype(BF16), wd_ref[...])
    pg = _sigmoid(_dot(_rms(x, gp_ref[...]).astype(BF16), wpg_ref[...]))
    x = x + _dot(p_ref[...].astype(BF16), wple_ref[...]) * pg
    o_ref[...] = _rms(x, gl_ref[...])


def _ffn_tail(x2, p2, g_ffn, g_ple, g_final, w_gate, w_up, w_down, w_ple, w_pg):
    n = x2.shape[0]
    tm = MATMUL_TILE
    row = lambda w: pl.BlockSpec((tm, w), lambda i: (i, 0))
    vec = _const_spec((1, D_MODEL))
    return pl.pallas_call(
        _ffn_tail_kernel,
        grid=(n // tm,),
        in_specs=[row(D_MODEL), row(PLE_DIM), vec, vec, vec,
                  _const_spec((D_MODEL, D_FF)), _const_spec((D_MODEL, D_FF)),
                  _const_spec((D_FF, D_MODEL)), _const_spec((PLE_DIM, D_MODEL)),
                  _const_spec((D_MODEL, D_MODEL))],
        out_specs=row(D_MODEL),
        out_shape=jax.ShapeDtypeStruct((n, D_MODEL), F32),
        compiler_params=_params(("parallel",)),
        name="ffn_tail",
    )(x2, p2, g_ffn, g_ple, g_final, w_gate, w_up, w_down, w_ple, w_pg)


def _chunk_cumsum_mats(tm):
    t = np.arange(tm)
    same = (t[:, None] // CHUNK) == (t[None, :] // CHUNK)
    fwd = same & (t[None, :] <= t[:, None])
    bwd = same & (t[None, :] >= t[:, None])
    return jnp.asarray(fwd, BF16), jnp.asarray(bwd, BF16)


def _lora_pair(m_f, m_b):
    z = jnp.zeros_like(m_f)
    return jnp.concatenate([jnp.concatenate([m_f, z], axis=1),
                            jnp.concatenate([z, m_b], axis=1)], axis=0).astype(BF16)


def _prepare_weights(g_mix, w_in, mu_prev, mu_next, w0_f, w2_f, w0_b, w2_b, a0_f, a2_f, a0_b, a2_b,
                     g2, k_k, k_a, r_k, lnx_w, lnx_b, rpb, w_br_a, w_br_n, w_out, g_ffn, w_gate,
                     w_up, w_down, g_ple, w_ple, w_pg, g_final):
    w_in = w_in[0].astype(BF16)
    head = np.arange(RWKV_WIDTH) // HEAD_DIM
    seg = (head[:, None] == head[None, :]).astype(np.float32)
    cum_f, cum_b = _chunk_cumsum_mats(TOKEN_TILE)
    vecs = jnp.stack([w0_f[0], w0_b[0], a0_f[0], a0_b[0], k_k[0], k_a[0], r_k[0].reshape(-1),
                      jnp.zeros((RWKV_WIDTH,), F32)])
    return dict(
        g_mix=g_mix[0][None], w_r=w_in[:, :RWKV_IN], w_n=w_in[:, RWKV_IN:RWKV_IN + NA_IN],
        w_g=w_in[:, RWKV_IN + NA_IN:], mu_p=mu_prev[0][None], mu_n=mu_next[0][None], vecs=vecs,
        wlora=_lora_pair(w2_f[0], w2_b[0]), alora=_lora_pair(a2_f[0], a2_b[0]),
        g2=g2[0].astype(BF16), seg=jnp.asarray(seg, BF16), segm=jnp.asarray(seg / HEAD_DIM, BF16),
        cum_f=cum_f, cum_b=cum_b, bias=_na_bias(rpb[0]),
        lnx_w=lnx_w[0][None], lnx_b=lnx_b[0][None], w_a=w_br_a[0].astype(BF16),
        w_nb=w_br_n[0].astype(BF16), w_o=w_out[0].astype(BF16), g_ffn=g_ffn[0][None],
        g_ple=g_ple[0][None], g_final=g_final[None], w_gate=w_gate[0].astype(BF16),
        w_up=w_up[0].astype(BF16), w_down=w_down[0].astype(BF16), w_ple=w_ple[0].astype(BF16),
        w_pg=w_pg[0].astype(BF16))


def _trunk(x, p, w):
    b, t, _ = x.shape
    n = b * t
    x2 = x.reshape(n, D_MODEL)
    z_r, qkv, gates = _in_proj(x2, w["g_mix"], w["w_r"], w["w_n"], w["w_g"])
    (r, v, kk, lin_f, lex_f, k_f, b_f, lin_b, lex_b, k_b, b_b, bonus, g) = _rwkv_prep(
        z_r.reshape(b, t, RWKV_IN), w["mu_p"], w["mu_n"], w["vecs"], w["wlora"], w["alora"], w["g2"],
        w["seg"], w["cum_f"], w["cum_b"])
    o_f = _rwkv_scan(r, v, kk, lin_f, lex_f, k_f, b_f, reverse=False)
    o_b = _rwkv_scan(r, v, kk, lin_b, lex_b, k_b, b_b, reverse=True)
    u_n = _na_attn(qkv.reshape(b, t, NA_IN), w["bias"])
    flat = lambda a: a.reshape(n, a.shape[-1])
    x1 = _merge(flat(o_f), flat(o_b), flat(bonus), flat(g), flat(u_n), gates, x2, w["lnx_w"], w["lnx_b"],
                w["segm"], w["w_a"], w["w_nb"], w["w_o"])
    y = _ffn_tail(x1, p.reshape(n, PLE_DIM), w["g_ffn"], w["g_ple"], w["g_final"], w["w_gate"],
                  w["w_up"], w["w_down"], w["w_ple"], w["w_pg"])
    return y.reshape(b, t, D_MODEL)


def kernel(x_prompt, x_sample, p_prompt, p_sample, g_mix, w_in, mu_prev, mu_next, w0_f, w2_f, w0_b, w2_b, a0_f, a2_f, a0_b, a2_b, g2, k_k, k_a, r_k, lnx_w, lnx_b, rpb, w_br_a, w_br_n, w_out, g_ffn, w_gate, w_up, w_down, g_ple, w_ple, w_pg, g_final):
    w = _prepare_weights(g_mix, w_in, mu_prev, mu_next, w0_f, w2_f, w0_b, w2_b, a0_f, a2_f, a0_b, a2_b,
                         g2, k_k, k_a, r_k, lnx_w, lnx_b, rpb, w_br_a, w_br_n, w_out, g_ffn, w_gate,
                         w_up, w_down, g_ple, w_ple, w_pg, g_final)
    return (_trunk(x_prompt, p_prompt[0], w), _trunk(x_sample, p_sample[0], w))
```
